```python
import math
import jax, jax.numpy as jnp
from jax import lax
import numpy as np

D_MODEL = 1024
BATCH = 16
SEQ = 2048
DEPTH = 1
DEC_BATCH = 128
DEC_SEQ = 4
PAST_LEN = 8192
PAGE_SIZE = 128

MIX_WIDTH = D_MODEL
ATTN_WIDTH = MIX_WIDTH // 2
CONV_WIDTH = MIX_WIDTH - ATTN_WIDTH
N_HEADS = 4
HEAD_DV = ATTN_WIDTH // N_HEADS
HEAD_DQK = HEAD_DV // 2
CONV_K = 3
Q_BLOCK = 128
RMS_EPS = 1e-6
SPLIT_SIZES = [N_HEADS * 2 * HEAD_DQK, N_HEADS * 2 * HEAD_DQK, ATTN_WIDTH, ATTN_WIDTH,
               CONV_WIDTH, CONV_WIDTH, CONV_WIDTH, CONV_WIDTH]
SPLIT_POINTS = [int(s) for s in np.cumsum(SPLIT_SIZES)[:-1]]
PROJ_WIDTH = int(sum(SPLIT_SIZES))
ATTN_SCALE = HEAD_DQK ** -0.5

kernel_name = "hymba_diffattn_shortconv_step"


def rms_norm(x, w):
    xf = x.astype(jnp.float32)
    y = xf * lax.rsqrt(jnp.mean(xf * xf, axis=-1, keepdims=True) + RMS_EPS)
    return (y * w.astype(jnp.float32)).astype(x.dtype)


def alibi_slopes(n):
    return jnp.asarray(np.array([2.0 ** (-8.0 * (i + 1) / n) for i in range(n)], dtype=np.float32))


def lambda_init_fn(layer):
    return 0.8 - 0.6 * math.exp(-0.3 * layer)


def project(x, norm_w, w_in, q_norm_w, k_norm_w):
    b, t = x.shape[:2]
    h = rms_norm(x, norm_w)
    p = h @ w_in
    q, k, v, gate_a, u, c_gate, b_gate, gate_c = jnp.split(p, SPLIT_POINTS, axis=-1)
    q = rms_norm(q.reshape(b, t, N_HEADS, 2, HEAD_DQK), q_norm_w)
    k = rms_norm(k.reshape(b, t, N_HEADS, 2, HEAD_DQK), k_norm_w)
    v = v.reshape(b, t, N_HEADS, HEAD_DV)
    z = c_gate * u
    return q, k, v, gate_a, b_gate, gate_c, z


def diff_logits(q, k, pos_q, pos_k, slopes):
    s = jnp.einsum('bthmd,bshmd->bhmts', q, k, preferred_element_type=jnp.float32) * ATTN_SCALE
    dist = (pos_q[:, None] - pos_k[None, :]).astype(jnp.float32)
    return s - slopes[:, None, None, None] * dist


def prompt_attention(q, k, v, slopes):
    b, t = q.shape[:2]
    nb = t // Q_BLOCK
    qb = q.reshape(b, nb, Q_BLOCK, N_HEADS, 2, HEAD_DQK).swapaxes(0, 1)
    pos_k = jnp.arange(t, dtype=jnp.int32)

    def one_block(args):
        q_blk, start = args
        pos_q = start + jnp.arange(Q_BLOCK, dtype=jnp.int32)
        s = diff_logits(q_blk, k, pos_q, pos_k, slopes)
        s = jnp.where(pos_q[:, None] >= pos_k[None, :], s, -jnp.inf)
        p = jax.nn.softmax(s, axis=-1)
        return jnp.einsum('bhmts,bshd->bhmtd', p.astype(v.dtype), v,
                          preferred_element_type=jnp.float32)

    starts = jnp.arange(nb, dtype=jnp.int32) * Q_BLOCK
    o = lax.map(one_block, (qb, starts))
    return o.transpose(1, 2, 3, 0, 4, 5).reshape(b, N_HEADS, 2, t, HEAD_DV)


def online_update(carry, s, v):
    m, l, acc = carry
    m_new = jnp.maximum(m, jnp.max(s, axis=-1))
    corr = jnp.exp(m - m_new)
    p = jnp.exp(s - m_new[..., None])
    l = l * corr + jnp.sum(p, axis=-1)
    acc = acc * corr[..., None] + jnp.einsum('bhmts,bshd->bhmtd', p.astype(v.dtype), v,
                                             preferred_element_type=jnp.float32)
    return (m_new, l, acc)


def sample_attention(q, k_new, v_new, cache_k, cache_v, layer, page_table, slopes):
    b, t = q.shape[:2]
    n_pages = page_table.shape[1]
    past = n_pages * PAGE_SIZE
    pos_q = past + jnp.arange(t, dtype=jnp.int32)
    init = (jnp.full((b, N_HEADS, 2, t), -jnp.inf, jnp.float32),
            jnp.zeros((b, N_HEADS, 2, t), jnp.float32),
            jnp.zeros((b, N_HEADS, 2, t, HEAD_DV), jnp.float32))

    def page_step(carry, xs):
        j, phys = xs
        k_pg = cache_k[layer, phys].reshape(b, PAGE_SIZE, N_HEADS, 2, HEAD_DQK)
        v_pg = cache_v[layer, phys]
        pos_k = j * PAGE_SIZE + jnp.arange(PAGE_SIZE, dtype=jnp.int32)
        s = diff_logits(q, k_pg, pos_q, pos_k, slopes)
        return online_update(carry, s, v_pg), None

    carry, _ = lax.scan(page_step, init,
                        (jnp.arange(n_pages, dtype=jnp.int32), page_table.T))
    s = diff_logits(q, k_new, pos_q, pos_q, slopes)
    s = jnp.where(pos_q[:, None] >= pos_q[None, :], s, -jnp.inf)
    _, l, acc = online_update(carry, s, v_new)
    return acc / l[..., None]


def diff_combine(o, lam, subln_w, lam_init, dtype):
    b, _, _, t, _ = o.shape
    d = o[:, :, 0] - lam * o[:, :, 1]
    d = d * lax.rsqrt(jnp.mean(d * d, axis=-1, keepdims=True) + RMS_EPS)
    d = d * subln_w.astype(jnp.float32) * (1.0 - lam_init)
    return d.transpose(0, 2, 1, 3).reshape(b, t, ATTN_WIDTH).astype(dtype)


def short_conv(zp, w):
    t = zp.shape[1] - (CONV_K - 1)
    y = w[0] * zp[:, 0:t]
    for j in range(1, CONV_K):
        y = y + w[j] * zp[:, j:j + t]
    return y


def finish(x, attn_o, gate_a, conv_y, b_gate, gate_c, w_out):
    a = attn_o * jax.nn.silu(gate_a)
    c = b_gate * conv_y * jax.nn.silu(gate_c)
    return x + jnp.concatenate([a, c], axis=-1) @ w_out


def setup_inputs(seed: int = 0) -> dict:
    key = jax.random.key(seed)
    ks = jax.random.split(key, 20)
    n_pages = PAST_LEN // PAGE_SIZE
    n_pool = (DEC_BATCH * n_pages * 5) // 4
    f32 = jnp.float32
    page_table = jax.random.permutation(ks[5], n_pool)[:DEC_BATCH * n_pages]
    page_table = page_table.reshape(DEC_BATCH, n_pages).astype(jnp.int32)
    return {
        "x_prompt": jax.random.normal(ks[0], (BATCH, SEQ, D_MODEL), f32),
        "x_sample": jax.random.normal(ks[1], (DEC_BATCH, DEC_SEQ, D_MODEL), f32),
        "cache_k": jax.random.normal(ks[2], (DEPTH, n_pool, PAGE_SIZE, N_HEADS, 2 * HEAD_DQK), f32),
        "cache_v": jax.random.normal(ks[3], (DEPTH, n_pool, PAGE_SIZE, N_HEADS, HEAD_DV), f32),
        "state_conv": jax.random.normal(ks[4], (DEPTH, DEC_BATCH, CONV_K - 1, CONV_WIDTH), f32),
        "page_table": page_table,
        "norm_w": 1.0 + 0.01 * jax.random.normal(ks[6], (DEPTH, D_MODEL), f32),
        "w_in": jax.random.normal(ks[7], (DEPTH, D_MODEL, PROJ_WIDTH), f32) * D_MODEL ** -0.5,
        "q_norm_w": 1.0 + 0.01 * jax.random.normal(ks[8], (DEPTH, HEAD_DQK), f32),
        "k_norm_w": 1.0 + 0.01 * jax.random.normal(ks[9], (DEPTH, HEAD_DQK), f32),
        "lambda_q1": 0.1 * jax.random.normal(ks[10], (DEPTH, HEAD_DQK), f32),
        "lambda_k1": 0.1 * jax.random.normal(ks[11], (DEPTH, HEAD_DQK), f32),
        "lambda_q2": 0.1 * jax.random.normal(ks[12], (DEPTH, HEAD_DQK), f32),
        "lambda_k2": 0.1 * jax.random.normal(ks[13], (DEPTH, HEAD_DQK), f32),
        "subln_w": 1.0 + 0.01 * jax.random.normal(ks[14], (DEPTH, HEAD_DV), f32),
        "conv_w": jax.random.normal(ks[15], (DEPTH, CONV_K, CONV_WIDTH), f32) * CONV_K ** -0.5,
        "w_out": jax.random.normal(ks[16], (DEPTH, MIX_WIDTH, D_MODEL), f32) * MIX_WIDTH ** -0.5,
    }


def reference(x_prompt, x_sample, cache_k, cache_v, state_conv, page_table, norm_w, w_in,
              q_norm_w, k_norm_w, lambda_q1, lambda_k1, lambda_q2, lambda_k2, subln_w,
              conv_w, w_out):
    slopes = alibi_slopes(N_HEADS)
    xp, xs = x_prompt, x_sample
    kp_l, vp_l, cp_l, ks_l, vs_l, cs_l = [], [], [], [], [], []
    for layer in range(DEPTH):
        lam_init = lambda_init_fn(layer)
        lam = (jnp.exp(jnp.sum(lambda_q1[layer].astype(jnp.float32) * lambda_k1[layer].astype(jnp.float32)))
               - jnp.exp(jnp.sum(lambda_q2[layer].astype(jnp.float32) * lambda_k2[layer].astype(jnp.float32)))
               + lam_init)

        b, t = xp.shape[:2]
        q, k, v, ga, bg, gc, z = project(xp, norm_w[layer], w_in[layer], q_norm_w[layer], k_norm_w[layer])
        o = prompt_attention(q, k, v, slopes)
        a = diff_combine(o, lam, subln_w[layer], lam_init, xp.dtype)
        zp = jnp.concatenate([jnp.zeros((b, CONV_K - 1, CONV_WIDTH), z.dtype), z], axis=1)
        cy = short_conv(zp, conv_w[layer])
        xp = finish(xp, a, ga, cy, bg, gc, w_out[layer])
        kp_l.append(k.reshape(b, t, N_HEADS, 2 * HEAD_DQK))
        vp_l.append(v)
        cp_l.append(zp[:, -(CONV_K - 1):])

        b, t = xs.shape[:2]
        q, k, v, ga, bg, gc, z = project(xs, norm_w[layer], w_in[layer], q_norm_w[layer], k_norm_w[layer])
        o = sample_attention(q, k, v, cache_k, cache_v, layer, page_table, slopes)
        a = diff_combine(o, lam, subln_w[layer], lam_init, xs.dtype)
        zs = jnp.concatenate([state_conv[layer].astype(z.dtype), z], axis=1)
        cy = short_conv(zs, conv_w[layer])
        xs = finish(xs, a, ga, cy, bg, gc, w_out[layer])
        ks_l.append(k.reshape(b, t, N_HEADS, 2 * HEAD_DQK))
        vs_l.append(v)
        cs_l.append(zs[:, -(CONV_K - 1):])

    return (xp, xs, jnp.stack(kp_l), jnp.stack(vp_l), jnp.stack(cp_l),
            jnp.stack(ks_l), jnp.stack(vs_l), jnp.stack(cs_l))
```

```python
import functools
import math

import jax
import jax.numpy as jnp
import numpy as np
from jax import lax
from jax.experimental import pallas as pl
from jax.experimental.pallas import tpu as pltpu

D_MODEL = 1024
N_HEADS = 4
HEAD_DV = 128
HEAD_DQK = 64
ATTN_WIDTH = N_HEADS * HEAD_DV
CONV_WIDTH = 512
CONV_K = 3
PAGE_SIZE = 128
RMS_EPS = 1e-6
ATTN_SCALE = HEAD_DQK ** -0.5
N_SPLITS = 8
SPLIT_W = 512
NEG_BIG = -1e30

VMEM_LIMIT_BYTES = 56 * 1024 * 1024
PROJ_TM = 512
ATTN_TQ = 256
ATTN_TK = 256
PAGES_PER_STEP = 8

_NT = (((1,), (1,)), ((), ()))


def _lambda_init(layer):
    return 0.8 - 0.6 * math.exp(-0.3 * layer)


def _silu(x):
    return x / (1.0 + jnp.exp(-x))


def _head_slope(h):
    return jnp.where(h == 0, 0.25, jnp.where(h == 1, 0.0625, jnp.where(h == 2, 0.015625, 0.00390625)))


def _lambda_value(lq1, lk1, lq2, lk2, lam_init):
    s1 = jnp.sum(lq1 * lk1, axis=-1, keepdims=True)
    s2 = jnp.sum(lq2 * lk2, axis=-1, keepdims=True)
    return jnp.exp(s1) - jnp.exp(s2) + lam_init


def _proj_kernel(decode, x_ref, nw_ref, w_ref, g_ref, qw_ref, kw_ref, cw_ref, *rest):
    if decode:
        st1_ref, st2_ref = rest[:2]
        rest = rest[2:]
    q_ref, k_ref, v_ref, kb_ref, vb_ref, sga_ref, c_ref, tail_ref = rest[:8]
    zc_ref = None if decode else rest[8]

    x = x_ref[...]
    h = x * lax.rsqrt(jnp.mean(x * x, axis=-1, keepdims=True) + RMS_EPS) * nw_ref[...]
    hb = h.astype(jnp.bfloat16)

    def split(i):
        return jnp.dot(hb, w_ref[:, i * SPLIT_W:(i + 1) * SPLIT_W], preferred_element_type=jnp.float32)

    def qk_norm(y, w):
        ms = jnp.dot((y * y).astype(jnp.bfloat16), g_ref[...], preferred_element_type=jnp.float32)
        return y * lax.rsqrt(ms + RMS_EPS) * w

    q = qk_norm(split(0), qw_ref[...])
    q_ref[...] = (q * ATTN_SCALE).astype(jnp.bfloat16)
    k = qk_norm(split(1), kw_ref[...])
    k_ref[...] = k
    kb_ref[...] = k.astype(jnp.bfloat16)
    v = split(2)
    v_ref[...] = v
    vb_ref[...] = v.astype(jnp.bfloat16)
    sga_ref[...] = _silu(split(3)).astype(jnp.bfloat16)

    z = split(5) * split(4)
    tm = z.shape[0]
    row = lax.broadcasted_iota(jnp.int32, z.shape, 0)
    r1 = pltpu.roll(z, 1, 0)
    r2 = pltpu.roll(z, 2, 0)
    if decode:
        t = row % 4
        p1 = jnp.where(t == 0, st1_ref[...], r1)
        p2 = jnp.where(t < 2, st2_ref[...], r2)
        tail_ref[...] = z
    else:
        first = pl.program_id(1) == 0
        zc = jnp.where(first, 0.0, zc_ref[...])
        p1 = jnp.where(row == 0, zc[7:8], r1)
        p2 = jnp.where(row == 0, zc[6:7], jnp.where(row == 1, zc[7:8], r2))
        zc_ref[...] = z[tm - 8:]
        tail_ref[...] = z[tm - 2:]
    cw = cw_ref[...]
    cy = cw[2:3] * z + cw[1:2] * p1 + cw[0:1] * p2
    c_ref[...] = (split(6) * cy * _silu(split(7))).astype(jnp.bfloat16)


def _project(x2d, norm_w, w_in_bf, g_mat, qw, kw, conv_w, *, batch, seq, decode, states=None):
    t_total = x2d.shape[0]
    tm = PROJ_TM
    assert seq % tm == 0 or decode
    if decode:
        assert t_total == tm
        grid = (1, 1)
        nt = 1
    else:
        nt = seq // tm
        grid = (batch, nt)
    tok = lambda b, t: (b * nt + t, 0)
    const = lambda b, t: (0, 0)
    in_specs = [
        pl.BlockSpec((tm, D_MODEL), tok),
        pl.BlockSpec((1, D_MODEL), const),
        pl.BlockSpec((D_MODEL, N_SPLITS * SPLIT_W), const),
        pl.BlockSpec((SPLIT_W, SPLIT_W), const),
        pl.BlockSpec((1, SPLIT_W), const),
        pl.BlockSpec((1, SPLIT_W), const),
        pl.BlockSpec((CONV_K, CONV_WIDTH), const),
    ]
    args = [x2d, norm_w, w_in_bf, g_mat, qw, kw, conv_w]
    if decode:
        in_specs += [pl.BlockSpec((tm, CONV_WIDTH), tok)] * 2
        args += list(states)
    f32, bf16 = jnp.float32, jnp.bfloat16
    tok_spec = pl.BlockSpec((tm, SPLIT_W), tok)
    out_shape = [jax.ShapeDtypeStruct((t_total, SPLIT_W), d) for d in (bf16, f32, f32, bf16, bf16, bf16, bf16)]
    out_specs = [tok_spec] * 7
    if decode:
        out_shape.append(jax.ShapeDtypeStruct((t_total, CONV_WIDTH), f32))
        out_specs.append(tok_spec)
        scratch = []
    else:
        out_shape.append(jax.ShapeDtypeStruct((batch, CONV_K - 1, CONV_WIDTH), f32))
        out_specs.append(pl.BlockSpec((None, CONV_K - 1, CONV_WIDTH), lambda b, t: (b, 0, 0)))
        scratch = [pltpu.VMEM((8, CONV_WIDTH), f32)]
    return pl.pallas_call(
        functools.partial(_proj_kernel, decode),
        grid=grid,
        in_specs=in_specs,
        out_specs=out_specs,
        out_shape=out_shape,
        scratch_shapes=scratch,
        compiler_params=pltpu.CompilerParams(
            dimension_semantics=("arbitrary", "arbitrary"), vmem_limit_bytes=VMEM_LIMIT_BYTES),
        name="proj_decode" if decode else "proj_prompt",
    )(*args)


def _diff_combine(o1, o2, lam, subln_w, lam_init):
    d = o1 - lam * o2
    d = d * lax.rsqrt(jnp.mean(d * d, axis=-1, keepdims=True) + RMS_EPS)
    return d * subln_w * (1.0 - lam_init)


def _prompt_attn_kernel(lam_init, q_ref, k_ref, v_ref, lq1_ref, lk1_ref, lq2_ref, lk2_ref, sw_ref, a_ref):
    h = pl.program_id(1)
    qi = pl.program_id(2)
    tq, tk = ATTN_TQ, ATTN_TK
    slope = _head_slope(h).astype(jnp.float32)

    q = q_ref[...]
    lane = lax.broadcasted_iota(jnp.int32, q.shape, 1)
    zero = jnp.zeros_like(q)
    q2 = jnp.concatenate([jnp.where(lane < HEAD_DQK, q, zero), jnp.where(lane >= HEAD_DQK, q, zero)], axis=0)

    r = lax.broadcasted_iota(jnp.int32, (2 * tq, tk), 0) % tq
    c = lax.broadcasted_iota(jnp.int32, (2 * tq, tk), 1)
    rel = r - c
    sd = slope * rel.astype(jnp.float32)

    def update(j, carry, masked):
        m, l, acc = carry
        kt = k_ref[pl.ds(pl.multiple_of(j * tk, tk), tk), :]
        vt = v_ref[pl.ds(pl.multiple_of(j * tk, tk), tk), :]
        s = lax.dot_general(q2, kt, _NT, preferred_element_type=jnp.float32) - sd
        if masked:
            s = jnp.where(rel >= 0, s, NEG_BIG)
        off = slope * ((qi - j) * tk).astype(jnp.float32)
        m_new = jnp.maximum(m, jnp.max(s, axis=-1, keepdims=True) - off)
        p = jnp.exp(s - (m_new + off))
        corr = jnp.exp(m - m_new)
        l = l * corr + jnp.sum(p, axis=-1, keepdims=True)
        acc = acc * corr + jnp.dot(p.astype(jnp.bfloat16), vt, preferred_element_type=jnp.float32)
        return m_new, l, acc

    init = (jnp.full((2 * tq, 1), NEG_BIG, jnp.float32),
            jnp.zeros((2 * tq, 1), jnp.float32),
            jnp.zeros((2 * tq, HEAD_DV), jnp.float32))
    carry = lax.fori_loop(0, qi, lambda j, cr: update(j, cr, False), init)
    _, l, acc = update(qi, carry, True)
    o = acc / l
    lam = _lambda_value(lq1_ref[...], lk1_ref[...], lq2_ref[...], lk2_ref[...], lam_init)
    a_ref[...] = _diff_combine(o[:tq], o[tq:], lam, sw_ref[...], lam_init).astype(a_ref.dtype)


def _prompt_attention(q_bf, k_bf, v_bf, lq1, lk1, lq2, lk2, subln_w, *, batch, seq, lam_init):
    tq = ATTN_TQ
    assert ATTN_TQ == ATTN_TK and seq % tq == 0
    nq = seq // tq
    vec = pl.BlockSpec((1, HEAD_DQK), lambda b, h, i: (0, 0))
    return pl.pallas_call(
        functools.partial(_prompt_attn_kernel, lam_init),
        grid=(batch, N_HEADS, nq),
        in_specs=[
            pl.BlockSpec((tq, HEAD_DV), lambda b, h, i: (b * nq + i, h)),
            pl.BlockSpec((seq, HEAD_DV), lambda b, h, i: (b, h)),
            pl.BlockSpec((seq, HEAD_DV), lambda b, h, i: (b, h)),
            vec, vec, vec, vec,
            pl.BlockSpec((1, HEAD_DV), lambda b, h, i: (0, 0)),
        ],
        out_specs=pl.BlockSpec((tq, HEAD_DV), lambda b, h, i: (b * nq + i, h)),
        out_shape=jax.ShapeDtypeStruct((batch * seq, ATTN_WIDTH), jnp.bfloat16),
        compiler_params=pltpu.CompilerParams(
            dimension_semantics=("arbitrary", "arbitrary", "arbitrary"), vmem_limit_bytes=VMEM_LIMIT_BYTES),
        name="prompt_attn",
    )(q_bf, k_bf, v_bf, lq1, lk1, lq2, lk2, subln_w)


def _decode_attn_kernel(lam_init, n_chunks, past_len, pt_ref, q_ref, kn_ref, vn_ref,
                        lq1_ref, lk1_ref, lq2_ref, lk2_ref, sw_ref, *rest):
    pps = PAGES_PER_STEP
    k_refs = rest[:pps]
    v_refs = rest[pps:2 * pps]
    a_ref = rest[2 * pps]
    m_ref, l_ref, acc_ref = rest[2 * pps + 1:]
    del pt_ref
    chunk = pl.program_id(1)
    n_rows = 2 * N_HEADS * 4
    n_keys = PAGE_SIZE * N_HEADS

    @pl.when(chunk == 0)
    def _():
        m_ref[...] = jnp.full(m_ref.shape, NEG_BIG, jnp.float32)
        l_ref[...] = jnp.zeros(l_ref.shape, jnp.float32)
        acc_ref[...] = jnp.zeros(acc_ref.shape, jnp.float32)

    row = lax.broadcasted_iota(jnp.int32, (n_rows, 1), 0)
    row_head = (row // 4) % N_HEADS
    row_tok = row % 4
    row_slope = _head_slope(row_head).astype(jnp.float32)

    def online_update(s_list, v_list, off):
        m = m_ref[...]
        mt = functools.reduce(jnp.maximum, [jnp.max(s, axis=-1, keepdims=True) for s in s_list])
        m_new = jnp.maximum(m, mt - off)
        shift = m_new + off
        corr = jnp.exp(m - m_new)
        lsum = jnp.zeros_like(m)
        pv = jnp.zeros(acc_ref.shape, jnp.float32)
        for s, vb in zip(s_list, v_list):
            p = jnp.exp(s - shift)
            lsum = lsum + jnp.sum(p, axis=-1, keepdims=True)
            pv = pv + jnp.dot(p.astype(jnp.bfloat16), vb, preferred_element_type=jnp.float32)
        m_ref[...] = m_new
        l_ref[...] = l_ref[...] * corr + lsum
        acc_ref[...] = acc_ref[...] * corr + pv

    q = q_ref[...]
    key = lax.broadcasted_iota(jnp.int32, (n_rows, n_keys), 1)
    key_head = key % N_HEADS
    key_tok = key // N_HEADS
    rel = (row_tok - key_tok).astype(jnp.float32)
    bias0 = jnp.where(key_head == row_head, -row_slope * rel, NEG_BIG)

    s_list, v_list = [], []
    for i in range(pps):
        kb = k_refs[i][...].astype(jnp.bfloat16)
        s = lax.dot_general(q, kb, _NT, preferred_element_type=jnp.float32)
        s_list.append(s + (bias0 + row_slope * float(i * PAGE_SIZE)))
        v_list.append(v_refs[i][...].astype(jnp.bfloat16))
    first_pos = (chunk * (pps * PAGE_SIZE)).astype(jnp.float32)
    online_update(s_list, v_list, row_slope * (float(past_len) - first_pos))

    @pl.when(chunk == n_chunks - 1)
    def _():
        nk = 4 * N_HEADS
        kcol = lax.broadcasted_iota(jnp.int32, (n_rows, nk), 1)
        kh = kcol % N_HEADS
        kt = kcol // N_HEADS
        ok = (kh == row_head) & (kt <= row_tok)
        bias = jnp.where(ok, -row_slope * (row_tok - kt).astype(jnp.float32), NEG_BIG)
        s = lax.dot_general(q, kn_ref[...].astype(jnp.bfloat16), _NT, preferred_element_type=jnp.float32)
        online_update([s + bias], [vn_ref[...].astype(jnp.bfloat16)], jnp.zeros((n_rows, 1), jnp.float32))
        o = acc_ref[...] / l_ref[...]
        lam = _lambda_value(lq1_ref[...], lk1_ref[...], lq2_ref[...], lk2_ref[...], lam_init)
        half = n_rows // 2
        a_ref[...] = _diff_combine(o[:half], o[half:], lam, sw_ref[...], lam_init)


def _decode_attention(page_table, qcat, k_new, v_new, cache_k3, cache_v3, lq1, lk1, lq2, lk2, subln_w,
                      *, layer_base, lam_init):
    dec_batch, n_pages = page_table.shape
    pps = PAGES_PER_STEP
    assert n_pages % pps == 0
    n_chunks = n_pages // pps
    n_keys = PAGE_SIZE * N_HEADS
    n_rows = qcat.shape[1]

    def page_spec(i):
        return pl.BlockSpec((None, n_keys, HEAD_DV),
                            lambda b, c, pt: (layer_base + pt[b, c * pps + i], 0, 0))

    per_seq = lambda rows: pl.BlockSpec((None, rows, HEAD_DV), lambda b, c, pt: (b, 0, 0))
    vec = pl.BlockSpec((1, HEAD_DQK), lambda b, c, pt: (0, 0))
    in_specs = ([per_seq(n_rows), per_seq(4 * N_HEADS), per_seq(4 * N_HEADS), vec, vec, vec, vec,
                 pl.BlockSpec((1, HEAD_DV), lambda b, c, pt: (0, 0))]
                + [page_spec(i) for i in range(pps)] * 2)
    grid_spec = pltpu.PrefetchScalarGridSpec(
        num_scalar_prefetch=1,
        grid=(dec_batch, n_chunks),
        in_specs=in_specs,
        out_specs=per_seq(n_rows // 2),
        scratch_shapes=[pltpu.VMEM((n_rows, 1), jnp.float32),
                        pltpu.VMEM((n_rows, 1), jnp.float32),
                        pltpu.VMEM((n_rows, HEAD_DV), jnp.float32)],
    )
    return pl.pallas_call(
        functools.partial(_decode_attn_kernel, lam_init, n_chunks, n_pages * PAGE_SIZE),
        grid_spec=grid_spec,
        out_shape=jax.ShapeDtypeStruct((dec_batch, n_rows // 2, HEAD_DV), jnp.float32),
        compiler_params=pltpu.CompilerParams(
            dimension_semantics=("arbitrary", "arbitrary"), vmem_limit_bytes=VMEM_LIMIT_BYTES),
        name="decode_attn",
    )(page_table, qcat, k_new, v_new, lq1, lk1, lq2, lk2, subln_w,
      *([cache_k3] * pps), *([cache_v3] * pps))


def _out_kernel(x_ref, a_ref, sga_ref, c_ref, wa_ref, wc_ref, y_ref):
    ag = (a_ref[...].astype(jnp.float32) * sga_ref[...].astype(jnp.float32)).astype(jnp.bfloat16)
    y = jnp.dot(ag, wa_ref[...], preferred_element_type=jnp.float32)
    y = y + jnp.dot(c_ref[...], wc_ref[...], preferred_element_type=jnp.float32)
    y_ref[...] = x_ref[...] + y


def _out_project(x2d, a, sga, c, wa, wc):
    t_total = x2d.shape[0]
    tm = PROJ_TM
    assert t_total % tm == 0
    tok = lambda i: (i, 0)
    const = lambda i: (0, 0)
    return pl.pallas_call(
        _out_kernel,
        grid=(t_total // tm,),
        in_specs=[
            pl.BlockSpec((tm, D_MODEL), tok),
            pl.BlockSpec((tm, ATTN_WIDTH), tok),
            pl.BlockSpec((tm, ATTN_WIDTH), tok),
            pl.BlockSpec((tm, CONV_WIDTH), tok),
            pl.BlockSpec((ATTN_WIDTH, D_MODEL), const),
            pl.BlockSpec((CONV_WIDTH, D_MODEL), const),
        ],
        out_specs=pl.BlockSpec((tm, D_MODEL), tok),
        out_shape=jax.ShapeDtypeStruct((t_total, D_MODEL), jnp.float32),
        compiler_params=pltpu.CompilerParams(
            dimension_semantics=("arbitrary",), vmem_limit_bytes=VMEM_LIMIT_BYTES),
        name="out_proj",
    )(x2d, a, sga, c, wa, wc)


def _group_mean_matrix():
    g = np.kron(np.eye(SPLIT_W // HEAD_DQK, dtype=np.float32),
                np.full((HEAD_DQK, HEAD_DQK), 1.0 / HEAD_DQK, np.float32))
    return jnp.asarray(g, jnp.bfloat16)


def kernel(x_prompt, x_sample, cache_k, cache_v, state_conv, page_table, norm_w, w_in, q_norm_w, k_norm_w,
           lambda_q1, lambda_k1, lambda_q2, lambda_k2, subln_w, conv_w, w_out):
    depth = norm_w.shape[0]
    batch, seq, _ = x_prompt.shape
    dec_batch, dec_seq, _ = x_sample.shape
    assert dec_seq == 4
    n_pool = cache_k.shape[1]
    n_keys = PAGE_SIZE * N_HEADS
    f32 = jnp.float32
    g_mat = _group_mean_matrix()
    cache_k3 = cache_k.reshape(depth * n_pool, n_keys, HEAD_DV)
    cache_v3 = cache_v.reshape(depth * n_pool, n_keys, HEAD_DV)

    xp = x_prompt.reshape(batch * seq, D_MODEL)
    xs = x_sample.reshape(dec_batch * dec_seq, D_MODEL)
    outs = [[] for _ in range(6)]
    for layer in range(depth):
        lam_init = _lambda_init(layer)
        w_in_bf = w_in[layer].astype(jnp.bfloat16)
        w_out_bf = w_out[layer].astype(jnp.bfloat16)
        wa, wc = w_out_bf[:ATTN_WIDTH], w_out_bf[ATTN_WIDTH:]
        nw = norm_w[layer].reshape(1, D_MODEL)
        qw = jnp.tile(q_norm_w[layer], SPLIT_W // HEAD_DQK).reshape(1, SPLIT_W)
        kw = jnp.tile(k_norm_w[layer], SPLIT_W // HEAD_DQK).reshape(1, SPLIT_W)
        lvecs = [v[layer].reshape(1, HEAD_DQK).astype(f32) for v in (lambda_q1, lambda_k1, lambda_q2, lambda_k2)]
        sw = subln_w[layer].reshape(1, HEAD_DV)
        cw = conv_w[layer]

        q_bf, k, v, k_bf, v_bf, sga, c, conv_tail = _project(
            xp, nw, w_in_bf, g_mat, qw, kw, cw, batch=batch, seq=seq, decode=False)
        a = _prompt_attention(q_bf, k_bf, v_bf, *lvecs, sw, batch=batch, seq=seq, lam_init=lam_init)
        xp = _out_project(xp, a, sga, c, wa, wc)
        outs[0].append(k.reshape(batch, seq, N_HEADS, 2 * HEAD_DQK))
        outs[1].append(v.reshape(batch, seq, N_HEADS, HEAD_DV))
        outs[2].append(conv_tail)

        st = state_conv[layer].astype(f32)
        zeros = jnp.zeros((dec_batch, 1, CONV_WIDTH), f32)
        st1 = jnp.concatenate([st[:, 1:2], zeros, zeros, zeros], axis=1).reshape(-1, CONV_WIDTH)
        st2 = jnp.concatenate([st[:, 0:1], st[:, 1:2], zeros, zeros], axis=1).reshape(-1, CONV_WIDTH)
        q_bf, k, v, _, _, sga, c, z = _project(
            xs, nw, w_in_bf, g_mat, qw, kw, cw, batch=dec_batch, seq=dec_seq, decode=True, states=(st1, st2))
        q5 = q_bf.reshape(dec_batch, dec_seq, N_HEADS, 2, HEAD_DQK).transpose(0, 3, 2, 1, 4)
        eye = jnp.eye(2, dtype=q_bf.dtype)
        qcat = (q5[:, :, :, :, None, :] * eye[None, :, None, None, :, None]).reshape(
            dec_batch, 2 * N_HEADS * dec_seq, 2 * HEAD_DQK)
        a = _decode_attention(
            page_table, qcat, k.reshape(dec_batch, dec_seq * N_HEADS, HEAD_DV),
            v.reshape(dec_batch, dec_seq * N_HEADS, HEAD_DV), cache_k3, cache_v3, *lvecs, sw,
            layer_base=layer * n_pool, lam_init=lam_init)
        a = a.reshape(dec_batch, N_HEADS, dec_seq, HEAD_DV).transpose(0, 2, 1, 3).reshape(-1, ATTN_WIDTH)
        xs = _out_project(xs, a, sga, c, wa, wc)
        outs[3].append(k.reshape(dec_batch, dec_seq, N_HEADS, 2 * HEAD_DQK))
        outs[4].append(v.reshape(dec_batch, dec_seq, N_HEADS, HEAD_DV))
        outs[5].append(z.reshape(dec_batch, dec_seq, CONV_WIDTH)[:, dec_seq - (CONV_K - 1):])

    return (xp.reshape(batch, seq, D_MODEL), xs.reshape(dec_batch, dec_seq, D_MODEL),
            jnp.stack(outs[0]), jnp.stack(outs[1]), jnp.stack(outs[2]),
            jnp.stack(outs[3]), jnp.stack(outs[4]), jnp.stack(outs[5]))
```

```python
import functools
import math

import jax
import jax.numpy as jnp
import numpy as np
from jax import lax
from jax.experimental import pallas as pl
from jax.experimental.pallas import tpu as pltpu

D_MODEL = 1024
N_HEADS = 4
HEAD_DV = 128
HEAD_DQK = 64
ATTN_WIDTH = N_HEADS * HEAD_DV
CONV_WIDTH = 512
CONV_K = 3
PAGE_SIZE = 128
DEC_SEQ = 4
RMS_EPS = 1e-6
ATTN_SCALE = HEAD_DQK ** -0.5
LOG2E = math.log2(math.e)
N_SPLITS = 8
SPLIT_W = 512
NEG_BIG = -1e30
SLOPES = tuple(2.0 ** (-8.0 * (h + 1) / N_HEADS) for h in range(N_HEADS))

VMEM_LIMIT_BYTES = 56 * 1024 * 1024
PROJ_TM = 512
ATTN_T = 256
ONES_ROWS = 16
PAGES_PER_STEP = 8
DECODE_RING = 3

_NT = (((1,), (1,)), ((), ()))


def _lambda_init(layer):
    return 0.8 - 0.6 * math.exp(-0.3 * layer)


def _silu(x):
    return x / (1.0 + jnp.exp(-x))


def _lambda_value(lq1, lk1, lq2, lk2, lam_init):
    s1 = jnp.sum(lq1 * lk1, axis=-1, keepdims=True)
    s2 = jnp.sum(lq2 * lk2, axis=-1, keepdims=True)
    return jnp.exp(s1) - jnp.exp(s2) + lam_init


def _proj_kernel(decode, x_ref, nw_ref, w_ref, g_ref, qw_ref, kw_ref, cw_ref, *rest):
    if decode:
        st1_ref, st2_ref = rest[:2]
        rest = rest[2:]
    q_ref, k_ref, v_ref, kb_ref, vb_ref, sga_ref, c_ref, tail_ref = rest[:8]
    zc_ref = None if decode else rest[8]

    x = x_ref[...]
    h = x * lax.rsqrt(jnp.mean(x * x, axis=-1, keepdims=True) + RMS_EPS) * nw_ref[...]
    hb = h.astype(jnp.bfloat16)

    def split(i):
        return jnp.dot(hb, w_ref[:, i * SPLIT_W:(i + 1) * SPLIT_W], preferred_element_type=jnp.float32)

    def qk_norm(y, w):
        ms = jnp.dot((y * y).astype(jnp.bfloat16), g_ref[...], preferred_element_type=jnp.float32)
        return y * lax.rsqrt(ms + RMS_EPS) * w

    q = qk_norm(split(0), qw_ref[...])
    q_ref[...] = (q * (ATTN_SCALE * LOG2E)).astype(jnp.bfloat16)
    tm = x.shape[0]

    def store_token_head_rows(o_ref, y):
        for hd in range(N_HEADS):
            o_ref[pl.ds(hd, tm, stride=N_HEADS), :] = y[:, hd * HEAD_DV:(hd + 1) * HEAD_DV]

    k = qk_norm(split(1), kw_ref[...])
    store_token_head_rows(k_ref, k)
    kb_ref[...] = k.astype(jnp.bfloat16)
    v = split(2)
    store_token_head_rows(v_ref, v)
    vb_ref[...] = v.astype(jnp.bfloat16)
    sga_ref[...] = _silu(split(3)).astype(jnp.bfloat16)

    z = split(5) * split(4)
    row = lax.broadcasted_iota(jnp.int32, z.shape, 0)
    r1 = pltpu.roll(z, 1, 0)
    r2 = pltpu.roll(z, 2, 0)
    if decode:
        t = row % DEC_SEQ
        p1 = jnp.where(t == 0, st1_ref[...], r1)
        p2 = jnp.where(t < 2, st2_ref[...], r2)
        tail_ref[...] = z
    else:
        first = pl.program_id(1) == 0
        zc = jnp.where(first, 0.0, zc_ref[...])
        p1 = jnp.where(row == 0, zc[7:8], r1)
        p2 = jnp.where(row == 0, zc[6:7], jnp.where(row == 1, zc[7:8], r2))
        zc_ref[...] = z[tm - 8:]
        tail_ref[...] = z[tm - 2:]
    cw = cw_ref[...]
    cy = cw[2:3] * z + cw[1:2] * p1 + cw[0:1] * p2
    c_ref[...] = (split(6) * cy * _silu(split(7))).astype(jnp.bfloat16)


def _project(x2d, norm_w, w_in_bf, g_mat, qw, kw, conv_w, *, batch, seq, decode, states=None):
    t_total = x2d.shape[0]
    tm = PROJ_TM
    assert seq % tm == 0 or decode
    if decode:
        assert t_total == tm
        grid = (1, 1)
        nt = 1
    else:
        nt = seq // tm
        grid = (batch, nt)
    tok = lambda b, t: (b * nt + t, 0)
    const = lambda b, t: (0, 0)
    in_specs = [
        pl.BlockSpec((tm, D_MODEL), tok),
        pl.BlockSpec((1, D_MODEL), const),
        pl.BlockSpec((D_MODEL, N_SPLITS * SPLIT_W), const),
        pl.BlockSpec((SPLIT_W, SPLIT_W), const),
        pl.BlockSpec((1, SPLIT_W), const),
        pl.BlockSpec((1, SPLIT_W), const),
        pl.BlockSpec((CONV_K, CONV_WIDTH), const),
    ]
    args = [x2d, norm_w, w_in_bf, g_mat, qw, kw, conv_w]
    if decode:
        in_specs += [pl.BlockSpec((tm, CONV_WIDTH), tok)] * 2
        args += list(states)
    f32, bf16 = jnp.float32, jnp.bfloat16
    tok_spec = pl.BlockSpec((tm, SPLIT_W), tok)
    th_spec = pl.BlockSpec((tm * N_HEADS, HEAD_DV), tok)
    th_shape = jax.ShapeDtypeStruct((t_total * N_HEADS, HEAD_DV), f32)
    out_shape = [jax.ShapeDtypeStruct((t_total, SPLIT_W), bf16), th_shape, th_shape] + [
        jax.ShapeDtypeStruct((t_total, SPLIT_W), bf16)] * 4
    out_specs = [tok_spec, th_spec, th_spec] + [tok_spec] * 4
    if decode:
        out_shape.append(jax.ShapeDtypeStruct((t_total, CONV_WIDTH), f32))
        out_specs.append(tok_spec)
        scratch = []
    else:
        out_shape.append(jax.ShapeDtypeStruct((batch, CONV_K - 1, CONV_WIDTH), f32))
        out_specs.append(pl.BlockSpec((None, CONV_K - 1, CONV_WIDTH), lambda b, t: (b, 0, 0)))
        scratch = [pltpu.VMEM((8, CONV_WIDTH), f32)]
    return pl.pallas_call(
        functools.partial(_proj_kernel, decode),
        grid=grid,
        in_specs=in_specs,
        out_specs=out_specs,
        out_shape=out_shape,
        scratch_shapes=scratch,
        compiler_params=pltpu.CompilerParams(
            dimension_semantics=("arbitrary", "arbitrary"), vmem_limit_bytes=VMEM_LIMIT_BYTES),
        name="proj_decode" if decode else "proj_prompt",
    )(*args)


def _prompt_attn_kernel(lam_init, q_ref, k_ref, v_ref, lq1_ref, lk1_ref, lq2_ref, lk2_ref, swc_ref, a_ref,
                        vt_ref, q2t_ref, sd_ref, m_ref, acc_ref, s0_ref, s1_ref):
    b = pl.program_id(0)
    qi = pl.program_id(1)
    t = ATTN_T
    n_kt = vt_ref.shape[1]
    f32, bf16 = jnp.float32, jnp.bfloat16

    @pl.when((b == 0) & (qi == 0))
    def _():
        r = lax.broadcasted_iota(jnp.int32, (t, 2 * t), 0)
        c = lax.broadcasted_iota(jnp.int32, (t, 2 * t), 1) % t
        rel = (c - r).astype(f32)
        for h in range(N_HEADS):
            sd = (SLOPES[h] * LOG2E) * rel
            sd_ref[0, h] = sd
            sd_ref[1, h] = jnp.where(rel >= 0.0, sd, -NEG_BIG)
            vt_ref[h, :, HEAD_DV:, :] = jnp.ones((n_kt, ONES_ROWS, t), bf16)

    @pl.when(qi == 0)
    def _():
        for h in range(N_HEADS):
            for c in range(n_kt):
                blk = v_ref[c * t:(c + 1) * t, h * HEAD_DV:(h + 1) * HEAD_DV].astype(f32)
                vt_ref[h, c, :HEAD_DV, :] = blk.T.astype(bf16)

    drow = lax.broadcasted_iota(jnp.int32, (2 * HEAD_DQK, t), 0)
    for h in range(N_HEADS):
        qt = q_ref[:, h * HEAD_DV:(h + 1) * HEAD_DV].astype(f32).T
        q2t_ref[h] = jnp.concatenate(
            [jnp.where(drow < HEAD_DQK, qt, 0.0), jnp.where(drow >= HEAD_DQK, qt, 0.0)], axis=1).astype(bf16)
        m_ref[h] = jnp.full(m_ref.shape[1:], NEG_BIG, f32)
        acc_ref[h] = jnp.zeros(acc_ref.shape[1:], f32)

    def scores(j, s_ref):
        koff = pl.multiple_of(j * t, t)
        diag = (j == qi).astype(jnp.int32)
        for h in range(N_HEADS):
            kt = k_ref[pl.ds(koff, t), h * HEAD_DV:(h + 1) * HEAD_DV]
            s_ref[h] = jnp.dot(kt, q2t_ref[h], preferred_element_type=f32) - sd_ref[diag, h]

    def accumulate(j, s_ref):
        dist_tiles = (qi - j).astype(f32)
        for h in range(N_HEADS):
            s = s_ref[h]
            off = (SLOPES[h] * LOG2E * t) * dist_tiles
            m = m_ref[h]
            m_new = jnp.maximum(m, jnp.max(s, axis=0, keepdims=True) - off)
            p = jnp.exp2(s - (m_new + off))
            corr = jnp.exp2(m - m_new)
            pv = jnp.dot(vt_ref[h, j], p.astype(bf16), preferred_element_type=f32)
            acc_ref[h] = acc_ref[h] * corr + pv
            m_ref[h] = m_new

    scores(0, s0_ref)

    def pair(i, carry):
        j = 2 * i
        scores(j + 1, s1_ref)
        accumulate(j, s0_ref)
        scores(jnp.minimum(j + 2, qi), s0_ref)
        accumulate(j + 1, s1_ref)
        return carry

    lax.fori_loop(0, (qi + 1) // 2, pair, 0)

    @pl.when(qi % 2 == 0)
    def _():
        accumulate(qi, s0_ref)

    lam = _lambda_value(lq1_ref[...], lk1_ref[...], lq2_ref[...], lk2_ref[...], lam_init)
    for h in range(N_HEADS):
        acc = acc_ref[h]
        o = acc[:HEAD_DV] / acc[HEAD_DV:HEAD_DV + 1]
        d = o[:, :t] - lam * o[:, t:]
        d = d * lax.rsqrt(jnp.mean(d * d, axis=0, keepdims=True) + RMS_EPS)
        d = d * swc_ref[...] * (1.0 - lam_init)
        a_ref[:, h * HEAD_DV:(h + 1) * HEAD_DV] = d.T.astype(a_ref.dtype)


def _prompt_attention(q_bf, k_bf, v_bf, lq1, lk1, lq2, lk2, subln_col, *, batch, seq, lam_init):
    t = ATTN_T
    assert seq % t == 0
    nq = seq // t
    vec = pl.BlockSpec((1, HEAD_DQK), lambda b, i: (0, 0))
    f32, bf16 = jnp.float32, jnp.bfloat16
    return pl.pallas_call(
        functools.partial(_prompt_attn_kernel, lam_init),
        grid=(batch, nq),
        in_specs=[
            pl.BlockSpec((t, ATTN_WIDTH), lambda b, i: (b * nq + i, 0)),
            pl.BlockSpec((seq, ATTN_WIDTH), lambda b, i: (b, 0)),
            pl.BlockSpec((seq, ATTN_WIDTH), lambda b, i: (b, 0)),
            vec, vec, vec, vec,
            pl.BlockSpec((HEAD_DV, 1), lambda b, i: (0, 0)),
        ],
        out_specs=pl.BlockSpec((t, ATTN_WIDTH), lambda b, i: (b * nq + i, 0)),
        out_shape=jax.ShapeDtypeStruct((batch * seq, ATTN_WIDTH), bf16),
        scratch_shapes=[
            pltpu.VMEM((N_HEADS, nq, HEAD_DV + ONES_ROWS, t), bf16),
            pltpu.VMEM((N_HEADS, 2 * HEAD_DQK, 2 * t), bf16),
            pltpu.VMEM((2, N_HEADS, t, 2 * t), f32),
            pltpu.VMEM((N_HEADS, 1, 2 * t), f32),
            pltpu.VMEM((N_HEADS, HEAD_DV + ONES_ROWS, 2 * t), f32),
            pltpu.VMEM((N_HEADS, t, 2 * t), f32),
            pltpu.VMEM((N_HEADS, t, 2 * t), f32),
        ],
        compiler_params=pltpu.CompilerParams(
            dimension_semantics=("arbitrary", "arbitrary"), vmem_limit_bytes=VMEM_LIMIT_BYTES),
        name="prompt_attn",
    )(q_bf, k_bf, v_bf, lq1, lk1, lq2, lk2, subln_col)


def _diff_combine(o1, o2, lam, subln_w, lam_init):
    d = o1 - lam * o2
    d = d * lax.rsqrt(jnp.mean(d * d, axis=-1, keepdims=True) + RMS_EPS)
    return d * subln_w * (1.0 - lam_init)


def _decode_attn_kernel(lam_init, n_chunks, n_steps, past_len, layer_base, pt_ref, q_ref, kn_ref, vn_ref,
                        lq1_ref, lk1_ref, lq2_ref, lk2_ref, sw_ref, ck_hbm, cv_hbm, a_ref,
                        kbuf, vbuf, sem, m_ref, l_ref, acc_ref):
    pps = PAGES_PER_STEP
    step = pl.program_id(0)
    chunk = step % n_chunks
    n_rows = 2 * N_HEADS * DEC_SEQ
    n_keys = PAGE_SIZE * N_HEADS
    f32, bf16 = jnp.float32, jnp.bfloat16

    def page_copies(st):
        slot = st % DECODE_RING
        copies = []
        for i in range(pps):
            page = layer_base + pt_ref[st * pps + i]
            copies.append(pltpu.make_async_copy(ck_hbm.at[page], kbuf.at[slot, i], sem.at[slot]))
            copies.append(pltpu.make_async_copy(cv_hbm.at[page], vbuf.at[slot, i], sem.at[slot]))
        return copies

    @pl.when(step == 0)
    def _():
        for st in range(DECODE_RING - 1):
            for cp in page_copies(st):
                cp.start()

    ahead = step + (DECODE_RING - 1)

    @pl.when(ahead < n_steps)
    def _():
        for cp in page_copies(ahead):
            cp.start()

    @pl.when(chunk == 0)
    def _():
        m_ref[...] = jnp.full(m_ref.shape, NEG_BIG, f32)
        l_ref[...] = jnp.zeros(l_ref.shape, f32)
        acc_ref[...] = jnp.zeros(acc_ref.shape, f32)

    row = lax.broadcasted_iota(jnp.int32, (n_rows, 1), 0)
    row_head = (row // DEC_SEQ) % N_HEADS
    row_tok = row % DEC_SEQ
    row_slope = jnp.where(row_head == 0, SLOPES[0], jnp.where(
        row_head == 1, SLOPES[1], jnp.where(row_head == 2, SLOPES[2], SLOPES[3]))).astype(f32) * LOG2E

    def online_update(s_list, v_list, off):
        m = m_ref[...]
        mt = functools.reduce(jnp.maximum, [jnp.max(s, axis=-1, keepdims=True) for s in s_list])
        m_new = jnp.maximum(m, mt - off)
        shift = m_new + off
        corr = jnp.exp2(m - m_new)
        lsum = jnp.zeros_like(m)
        pv = jnp.zeros(acc_ref.shape, f32)
        for s, vb in zip(s_list, v_list):
            p = jnp.exp2(s - shift)
            lsum = lsum + jnp.sum(p, axis=-1, keepdims=True)
            pv = pv + jnp.dot(p.astype(bf16), vb, preferred_element_type=f32)
        m_ref[...] = m_new
        l_ref[...] = l_ref[...] * corr + lsum
        acc_ref[...] = acc_ref[...] * corr + pv

    q = q_ref[...]
    key = lax.broadcasted_iota(jnp.int32, (n_rows, n_keys), 1)
    key_head = key % N_HEADS
    key_tok = key // N_HEADS
    rel = (row_tok - key_tok).astype(f32)
    bias0 = jnp.where(key_head == row_head, -row_slope * rel, NEG_BIG)

    for cp in page_copies(step):
        cp.wait()
    slot = step % DECODE_RING
    s_list, v_list = [], []
    for i in range(pps):
        kb = kbuf[slot, i].astype(bf16)
        s = lax.dot_general(q, kb, _NT, preferred_element_type=f32)
        s_list.append(s + (bias0 + row_slope * float(i * PAGE_SIZE)))
        v_list.append(vbuf[slot, i].astype(bf16))
    first_pos = (chunk * (pps * PAGE_SIZE)).astype(f32)
    online_update(s_list, v_list, row_slope * (float(past_len) - first_pos))

    @pl.when(chunk == n_chunks - 1)
    def _():
        nk = DEC_SEQ * N_HEADS
        kcol = lax.broadcasted_iota(jnp.int32, (n_rows, nk), 1)
        kh = kcol % N_HEADS
        kt = kcol // N_HEADS
        ok = (kh == row_head) & (kt <= row_tok)
        bias = jnp.where(ok, -row_slope * (row_tok - kt).astype(f32), NEG_BIG)
        s = lax.dot_general(q, kn_ref[...].astype(bf16), _NT, preferred_element_type=f32)
        online_update([s + bias], [vn_ref[...].astype(bf16)], jnp.zeros((n_rows, 1), f32))
        o = acc_ref[...] / l_ref[...]
        lam = _lambda_value(lq1_ref[...], lk1_ref[...], lq2_ref[...], lk2_ref[...], lam_init)
        half = n_rows // 2
        a_ref[...] = _diff_combine(o[:half], o[half:], lam, sw_ref[...], lam_init)


def _decode_attention(page_table, qcat, k_new, v_new, cache_k3, cache_v3, lq1, lk1, lq2, lk2, subln_w,
                      *, layer_base, lam_init):
    dec_batch, n_pages = page_table.shape
    pps = PAGES_PER_STEP
    assert n_pages % pps == 0
    n_chunks = n_pages // pps
    n_steps = dec_batch * n_chunks
    assert n_steps >= DECODE_RING - 1
    n_keys = PAGE_SIZE * N_HEADS
    n_rows = qcat.shape[1]
    f32 = jnp.float32

    per_seq = lambda rows: pl.BlockSpec((None, rows, HEAD_DV), lambda s, pt: (s // n_chunks, 0, 0))
    vec = pl.BlockSpec((1, HEAD_DQK), lambda s, pt: (0, 0))
    in_specs = [per_seq(n_rows), per_seq(DEC_SEQ * N_HEADS), per_seq(DEC_SEQ * N_HEADS), vec, vec, vec, vec,
                pl.BlockSpec((1, HEAD_DV), lambda s, pt: (0, 0)),
                pl.BlockSpec(memory_space=pl.ANY), pl.BlockSpec(memory_space=pl.ANY)]
    grid_spec = pltpu.PrefetchScalarGridSpec(
        num_scalar_prefetch=1,
        grid=(n_steps,),
        in_specs=in_specs,
        out_specs=per_seq(n_rows // 2),
        scratch_shapes=[pltpu.VMEM((DECODE_RING, pps, n_keys, HEAD_DV), f32),
                        pltpu.VMEM((DECODE_RING, pps, n_keys, HEAD_DV), f32),
                        pltpu.SemaphoreType.DMA((DECODE_RING,)),
                        pltpu.VMEM((n_rows, 1), f32),
                        pltpu.VMEM((n_rows, 1), f32),
                        pltpu.VMEM((n_rows, HEAD_DV), f32)],
    )
    return pl.pallas_call(
        functools.partial(_decode_attn_kernel, lam_init, n_chunks, n_steps, n_pages * PAGE_SIZE, layer_base),
        grid_spec=grid_spec,
        out_shape=jax.ShapeDtypeStruct((dec_batch, n_rows // 2, HEAD_DV), f32),
        compiler_params=pltpu.CompilerParams(
            dimension_semantics=("arbitrary",), vmem_limit_bytes=VMEM_LIMIT_BYTES),
        name="decode_attn",
    )(page_table.reshape(-1), qcat, k_new, v_new, lq1, lk1, lq2, lk2, subln_w, cache_k3, cache_v3)


def _out_kernel(x_ref, a_ref, sga_ref, c_ref, wa_ref, wc_ref, y_ref):
    ag = (a_ref[...].astype(jnp.float32) * sga_ref[...].astype(jnp.float32)).astype(jnp.bfloat16)
    y = jnp.dot(ag, wa_ref[...], preferred_element_type=jnp.float32)
    y = y + jnp.dot(c_ref[...], wc_ref[...], preferred_element_type=jnp.float32)
    y_ref[...] = x_ref[...] + y


def _out_project(x2d, a, sga, c, wa, wc):
    t_total = x2d.shape[0]
    tm = PROJ_TM
    assert t_total % tm == 0
    tok = lambda i: (i, 0)
    const = lambda i: (0, 0)
    return pl.pallas_call(
        _out_kernel,
        grid=(t_total // tm,),
        in_specs=[
            pl.BlockSpec((tm, D_MODEL), tok),
            pl.BlockSpec((tm, ATTN_WIDTH), tok),
            pl.BlockSpec((tm, ATTN_WIDTH), tok),
            pl.BlockSpec((tm, CONV_WIDTH), tok),
            pl.BlockSpec((ATTN_WIDTH, D_MODEL), const),
            pl.BlockSpec((CONV_WIDTH, D_MODEL), const),
        ],
        out_specs=pl.BlockSpec((tm, D_MODEL), tok),
        out_shape=jax.ShapeDtypeStruct((t_total, D_MODEL), jnp.float32),
        compiler_params=pltpu.CompilerParams(
            dimension_semantics=("arbitrary",), vmem_limit_bytes=VMEM_LIMIT_BYTES),
        name="out_proj",
    )(x2d, a, sga, c, wa, wc)


def _group_mean_matrix():
    g = np.kron(np.eye(SPLIT_W // HEAD_DQK, dtype=np.float32),
                np.full((HEAD_DQK, HEAD_DQK), 1.0 / HEAD_DQK, np.float32))
    return jnp.asarray(g, jnp.bfloat16)


def kernel(x_prompt, x_sample, cache_k, cache_v, state_conv, page_table, norm_w, w_in, q_norm_w, k_norm_w,
           lambda_q1, lambda_k1, lambda_q2, lambda_k2, subln_w, conv_w, w_out):
    depth = norm_w.shape[0]
    batch, seq, _ = x_prompt.shape
    dec_batch, dec_seq, _ = x_sample.shape
    assert dec_seq == DEC_SEQ
    n_pool = cache_k.shape[1]
    n_keys = PAGE_SIZE * N_HEADS
    f32 = jnp.float32
    g_mat = _group_mean_matrix()
    cache_k3 = cache_k.reshape(depth * n_pool, n_keys, HEAD_DV)
    cache_v3 = cache_v.reshape(depth * n_pool, n_keys, HEAD_DV)

    xp = x_prompt.reshape(batch * seq, D_MODEL)
    xs = x_sample.reshape(dec_batch * dec_seq, D_MODEL)
    outs = [[] for _ in range(6)]
    for layer in range(depth):
        lam_init = _lambda_init(layer)
        w_in_bf = w_in[layer].astype(jnp.bfloat16)
        w_out_bf = w_out[layer].astype(jnp.bfloat16)
        wa, wc = w_out_bf[:ATTN_WIDTH], w_out_bf[ATTN_WIDTH:]
        nw = norm_w[layer].reshape(1, D_MODEL)
        qw = jnp.tile(q_norm_w[layer], SPLIT_W // HEAD_DQK).reshape(1, SPLIT_W)
        kw = jnp.tile(k_norm_w[layer], SPLIT_W // HEAD_DQK).reshape(1, SPLIT_W)
        lvecs = [v[layer].reshape(1, HEAD_DQK).astype(f32) for v in (lambda_q1, lambda_k1, lambda_q2, lambda_k2)]
        sw = subln_w[layer].astype(f32)
        cw = conv_w[layer]

        q_bf, k, v, k_bf, v_bf, sga, c, conv_tail = _project(
            xp, nw, w_in_bf, g_mat, qw, kw, cw, batch=batch, seq=seq, decode=False)
        a = _prompt_attention(q_bf, k_bf, v_bf, *lvecs, sw.reshape(HEAD_DV, 1),
                              batch=batch, seq=seq, lam_init=lam_init)
        xp = _out_project(xp, a, sga, c, wa, wc)
        outs[0].append(k.reshape(batch, seq, N_HEADS, 2 * HEAD_DQK))
        outs[1].append(v.reshape(batch, seq, N_HEADS, HEAD_DV))
        outs[2].append(conv_tail)

        st = state_conv[layer].astype(f32)
        zeros = jnp.zeros((dec_batch, 1, CONV_WIDTH), f32)
        st1 = jnp.concatenate([st[:, 1:2], zeros, zeros, zeros], axis=1).reshape(-1, CONV_WIDTH)
        st2 = jnp.concatenate([st[:, 0:1], st[:, 1:2], zeros, zeros], axis=1).reshape(-1, CONV_WIDTH)
        q_bf, k, v, _, _, sga, c, z = _project(
            xs, nw, w_in_bf, g_mat, qw, kw, cw, batch=dec_batch, seq=dec_seq, decode=True, states=(st1, st2))
        q5 = q_bf.reshape(dec_batch, dec_seq, N_HEADS, 2, HEAD_DQK).transpose(0, 3, 2, 1, 4)
        eye = jnp.eye(2, dtype=q_bf.dtype)
        qcat = (q5[:, :, :, :, None, :] * eye[None, :, None, None, :, None]).reshape(
            dec_batch, 2 * N_HEADS * dec_seq, 2 * HEAD_DQK)
        a = _decode_attention(
            page_table, qcat, k.reshape(dec_batch, dec_seq * N_HEADS, HEAD_DV),
            v.reshape(dec_batch, dec_seq * N_HEADS, HEAD_DV), cache_k3, cache_v3, *lvecs,
            sw.reshape(1, HEAD_DV), layer_base=layer * n_pool, lam_init=lam_init)
        a = a.reshape(dec_batch, N_HEADS, dec_seq, HEAD_DV).transpose(0, 2, 1, 3).reshape(-1, ATTN_WIDTH)
        xs = _out_project(xs, a, sga, c, wa, wc)
        outs[3].append(k.reshape(dec_batch, dec_seq, N_HEADS, 2 * HEAD_DQK))
        outs[4].append(v.reshape(dec_batch, dec_seq, N_HEADS, HEAD_DV))
        outs[5].append(z.reshape(dec_batch, dec_seq, CONV_WIDTH)[:, dec_seq - (CONV_K - 1):])

    return (xp.reshape(batch, seq, D_MODEL), xs.reshape(dec_batch, dec_seq, D_MODEL),
            jnp.stack(outs[0]), jnp.stack(outs[1]), jnp.stack(outs[2]),
            jnp.stack(outs[3]), jnp.stack(outs[4]), jnp.stack(outs[5]))
```

```python
import functools
import math

import jax
import jax.numpy as jnp
import numpy as np
from jax import lax
from jax.experimental import pallas as pl
from jax.experimental.pallas import tpu as pltpu

D_MODEL = 1024
N_HEADS = 4
HEAD_DV = 128
HEAD_DQK = 64
ATTN_WIDTH = N_HEADS * HEAD_DV
CONV_WIDTH = 512
CONV_K = 3
PAGE_SIZE = 128
DEC_SEQ = 4
RMS_EPS = 1e-6
ATTN_SCALE = HEAD_DQK ** -0.5
LOG2E = math.log2(math.e)
N_SPLITS = 8
SPLIT_W = 512
NEG_BIG = -1e30
SLOPES = tuple(2.0 ** (-8.0 * (h + 1) / N_HEADS) for h in range(N_HEADS))

MXU_TILE = 256
VMEM_LIMIT_BYTES = 56 * 1024 * 1024
PROJ_TM = 512
ATTN_T = 256
ONES_ROWS = 16
PAGES_PER_CHUNK = 8
DECODE_RING = 4

_NT = (((1,), (1,)), ((), ()))


def _lambda_init(layer):
    return 0.8 - 0.6 * math.exp(-0.3 * layer)


def _silu(x):
    return x / (1.0 + jnp.exp(-x))


def _lambda_value(lq1, lk1, lq2, lk2, lam_init):
    s1 = jnp.sum(lq1 * lk1, axis=-1, keepdims=True)
    s2 = jnp.sum(lq2 * lk2, axis=-1, keepdims=True)
    return jnp.exp(s1) - jnp.exp(s2) + lam_init


def _proj_kernel(decode, x_ref, nw_ref, w_ref, g_ref, qw_ref, kw_ref, cw_ref, *rest):
    if decode:
        st1_ref, st2_ref = rest[:2]
        rest = rest[2:]
    q_ref, k_ref, v_ref, kb_ref, vb_ref, sga_ref, c_ref, tail_ref = rest[:8]
    zc_ref = None if decode else rest[8]

    x = x_ref[...]
    h = x * lax.rsqrt(jnp.mean(x * x, axis=-1, keepdims=True) + RMS_EPS) * nw_ref[...]
    hb = h.astype(jnp.bfloat16)

    def split(i):
        return jnp.dot(hb, w_ref[:, i * SPLIT_W:(i + 1) * SPLIT_W], preferred_element_type=jnp.float32)

    def qk_norm(y, w):
        yy = (y * y).astype(jnp.bfloat16)
        ms = jnp.concatenate(
            [jnp.dot(yy[:, i * MXU_TILE:(i + 1) * MXU_TILE], g_ref[...], preferred_element_type=jnp.float32)
             for i in range(SPLIT_W // MXU_TILE)], axis=1)
        return y * lax.rsqrt(ms + RMS_EPS) * w

    q = qk_norm(split(0), qw_ref[...])
    q_ref[...] = (q * (ATTN_SCALE * LOG2E)).astype(jnp.bfloat16)
    tm = x.shape[0]

    def store_token_head_rows(o_ref, y):
        for hd in range(N_HEADS):
            o_ref[pl.ds(hd, tm, stride=N_HEADS), :] = y[:, hd * HEAD_DV:(hd + 1) * HEAD_DV]

    k = qk_norm(split(1), kw_ref[...])
    store_token_head_rows(k_ref, k)
    kb_ref[...] = k.astype(jnp.bfloat16)
    v = split(2)
    store_token_head_rows(v_ref, v)
    vb_ref[...] = v.astype(jnp.bfloat16)
    sga_ref[...] = _silu(split(3)).astype(jnp.bfloat16)

    z = split(5) * split(4)
    row = lax.broadcasted_iota(jnp.int32, z.shape, 0)
    r1 = pltpu.roll(z, 1, 0)
    r2 = pltpu.roll(z, 2, 0)
    if decode:
        t = row % DEC_SEQ
        p1 = jnp.where(t == 0, st1_ref[...], r1)
        p2 = jnp.where(t < 2, st2_ref[...], r2)
        tail_ref[...] = z
    else:
        first = pl.program_id(1) == 0
        zc = jnp.where(first, 0.0, zc_ref[...])
        p1 = jnp.where(row == 0, zc[7:8], r1)
        p2 = jnp.where(row == 0, zc[6:7], jnp.where(row == 1, zc[7:8], r2))
        zc_ref[...] = z[tm - 8:]
        tail_ref[...] = z[tm - 2:]
    cw = cw_ref[...]
    cy = cw[2:3] * z + cw[1:2] * p1 + cw[0:1] * p2
    c_ref[...] = (split(6) * cy * _silu(split(7))).astype(jnp.bfloat16)


def _project(x2d, norm_w, w_in_bf, g_mat, qw, kw, conv_w, *, batch, seq, decode, states=None):
    t_total = x2d.shape[0]
    tm = PROJ_TM
    assert seq % tm == 0 or decode
    if decode:
        assert t_total == tm
        grid = (1, 1)
        nt = 1
    else:
        nt = seq // tm
        grid = (batch, nt)
    tok = lambda b, t: (b * nt + t, 0)
    const = lambda b, t: (0, 0)
    in_specs = [
        pl.BlockSpec((tm, D_MODEL), tok),
        pl.BlockSpec((1, D_MODEL), const),
        pl.BlockSpec((D_MODEL, N_SPLITS * SPLIT_W), const),
        pl.BlockSpec((MXU_TILE, MXU_TILE), const),
        pl.BlockSpec((1, SPLIT_W), const),
        pl.BlockSpec((1, SPLIT_W), const),
        pl.BlockSpec((CONV_K, CONV_WIDTH), const),
    ]
    args = [x2d, norm_w, w_in_bf, g_mat, qw, kw, conv_w]
    if decode:
        in_specs += [pl.BlockSpec((tm, CONV_WIDTH), tok)] * 2
        args += list(states)
    f32, bf16 = jnp.float32, jnp.bfloat16
    tok_spec = pl.BlockSpec((tm, SPLIT_W), tok)
    th_spec = pl.BlockSpec((tm * N_HEADS, HEAD_DV), tok)
    th_shape = jax.ShapeDtypeStruct((t_total * N_HEADS, HEAD_DV), f32)
    out_shape = [jax.ShapeDtypeStruct((t_total, SPLIT_W), bf16), th_shape, th_shape] + [
        jax.ShapeDtypeStruct((t_total, SPLIT_W), bf16)] * 4
    out_specs = [tok_spec, th_spec, th_spec] + [tok_spec] * 4
    if decode:
        out_shape.append(jax.ShapeDtypeStruct((t_total, CONV_WIDTH), f32))
        out_specs.append(tok_spec)
        scratch = []
    else:
        out_shape.append(jax.ShapeDtypeStruct((batch, CONV_K - 1, CONV_WIDTH), f32))
        out_specs.append(pl.BlockSpec((None, CONV_K - 1, CONV_WIDTH), lambda b, t: (b, 0, 0)))
        scratch = [pltpu.VMEM((8, CONV_WIDTH), f32)]
    return pl.pallas_call(
        functools.partial(_proj_kernel, decode),
        grid=grid,
        in_specs=in_specs,
        out_specs=out_specs,
        out_shape=out_shape,
        scratch_shapes=scratch,
        compiler_params=pltpu.CompilerParams(
            dimension_semantics=("arbitrary", "arbitrary"), vmem_limit_bytes=VMEM_LIMIT_BYTES),
        name="proj_decode" if decode else "proj_prompt",
    )(*args)


def _diff_combine(o1, o2, lam, subln_w, lam_init):
    d = o1 - lam * o2
    d = d * lax.rsqrt(jnp.mean(d * d, axis=-1, keepdims=True) + RMS_EPS)
    return d * subln_w * (1.0 - lam_init)


def _attn_kernel(lam_init, n_chunks, past_len, layer_base,
                 pt_ref, q_ref, k_ref, v_ref, lq1_ref, lk1_ref, lq2_ref, lk2_ref, swc_ref, sw_ref,
                 x_ref, sga_ref, c_ref, wa_ref, wc_ref, qd_ref, kn_ref, vn_ref, ck_hbm, cv_hbm,
                 y_ref, ad_ref,
                 vt_ref, q2t_ref, sd_ref, m_ref, acc_ref, s0_ref, s1_ref, ag_ref,
                 kbuf, vbuf, sem, dm_ref, dl_ref, dacc_ref):
    b = pl.program_id(0)
    qi = pl.program_id(1)
    n_q = pl.num_programs(1)
    step = b * n_q + qi
    n_total = pl.num_programs(0) * n_q * n_chunks
    t = ATTN_T
    n_kt = vt_ref.shape[1]
    ppc = PAGES_PER_CHUNK
    f32, bf16 = jnp.float32, jnp.bfloat16

    def chunk_copies(g):
        slot = g % DECODE_RING
        copies = []
        for i in range(ppc):
            page = layer_base + pt_ref[g * ppc + i]
            copies.append(pltpu.make_async_copy(ck_hbm.at[page], kbuf.at[slot, i], sem.at[slot]))
            copies.append(pltpu.make_async_copy(cv_hbm.at[page], vbuf.at[slot, i], sem.at[slot]))
        return copies

    @pl.when(step == 0)
    def _():
        for g in range(DECODE_RING - 1):
            for cp in chunk_copies(g):
                cp.start()

    @pl.when(step == 0)
    def _():
        r = lax.broadcasted_iota(jnp.int32, (t, 2 * t), 0)
        c = lax.broadcasted_iota(jnp.int32, (t, 2 * t), 1) % t
        rel = (c - r).astype(f32)
        for h in range(N_HEADS):
            sd = (SLOPES[h] * LOG2E) * rel
            sd_ref[0, h] = sd
            sd_ref[1, h] = jnp.where(rel >= 0.0, sd, -NEG_BIG)
            vt_ref[h, :, HEAD_DV:, :] = jnp.ones((n_kt, ONES_ROWS, t), bf16)

    @pl.when(qi == 0)
    def _():
        for h in range(N_HEADS):
            for c in range(n_kt):
                blk = v_ref[c * t:(c + 1) * t, h * HEAD_DV:(h + 1) * HEAD_DV].astype(f32)
                vt_ref[h, c, :HEAD_DV, :] = blk.T.astype(bf16)

    drow = lax.broadcasted_iota(jnp.int32, (2 * HEAD_DQK, t), 0)
    for h in range(N_HEADS):
        qt = q_ref[:, h * HEAD_DV:(h + 1) * HEAD_DV].astype(f32).T
        q2t_ref[h] = jnp.concatenate(
            [jnp.where(drow < HEAD_DQK, qt, 0.0), jnp.where(drow >= HEAD_DQK, qt, 0.0)], axis=1).astype(bf16)
        m_ref[h] = jnp.full(m_ref.shape[1:], NEG_BIG, f32)
        acc_ref[h] = jnp.zeros(acc_ref.shape[1:], f32)

    def scores(j, s_ref):
        koff = pl.multiple_of(j * t, t)
        diag = (j == qi).astype(jnp.int32)
        for h in range(N_HEADS):
            kt = k_ref[pl.ds(koff, t), h * HEAD_DV:(h + 1) * HEAD_DV]
            s_ref[h] = jnp.dot(kt, q2t_ref[h], preferred_element_type=f32) - sd_ref[diag, h]

    def accumulate(j, s_ref):
        dist_tiles = (qi - j).astype(f32)
        for h in range(N_HEADS):
            s = s_ref[h]
            off = (SLOPES[h] * LOG2E * t) * dist_tiles
            m = m_ref[h]
            m_new = jnp.maximum(m, jnp.max(s, axis=0, keepdims=True) - off)
            p = jnp.exp2(s - (m_new + off))
            corr = jnp.exp2(m - m_new)
            pv = jnp.dot(vt_ref[h, j], p.astype(bf16), preferred_element_type=f32)
            acc_ref[h] = acc_ref[h] * corr + pv
            m_ref[h] = m_new

    scores(0, s0_ref)

    def pair(i, carry):
        j = 2 * i
        scores(j + 1, s1_ref)
        accumulate(j, s0_ref)
        scores(jnp.minimum(j + 2, qi), s0_ref)
        accumulate(j + 1, s1_ref)
        return carry

    lax.fori_loop(0, (qi + 1) // 2, pair, 0)

    @pl.when(qi % 2 == 0)
    def _():
        accumulate(qi, s0_ref)

    lam = _lambda_value(lq1_ref[...], lk1_ref[...], lq2_ref[...], lk2_ref[...], lam_init)
    for h in range(N_HEADS):
        acc = acc_ref[h]
        o = acc[:HEAD_DV] / acc[HEAD_DV:HEAD_DV + 1]
        d = o[:, :t] - lam * o[:, t:]
        d = d * lax.rsqrt(jnp.mean(d * d, axis=0, keepdims=True) + RMS_EPS)
        d = d * swc_ref[...] * (1.0 - lam_init)
        hs = slice(h * HEAD_DV, (h + 1) * HEAD_DV)
        ag_ref[:, hs] = (d.T * sga_ref[:, hs].astype(f32)).astype(bf16)

    y = jnp.dot(ag_ref[...], wa_ref[...], preferred_element_type=f32)
    y = y + jnp.dot(c_ref[...], wc_ref[...], preferred_element_type=f32)
    y_ref[...] = x_ref[...] + y

    n_rows = 2 * N_HEADS * DEC_SEQ
    n_keys = PAGE_SIZE * N_HEADS
    row = lax.broadcasted_iota(jnp.int32, (n_rows, 1), 0)
    row_head = (row // DEC_SEQ) % N_HEADS
    row_tok = row % DEC_SEQ
    row_slope = jnp.where(row_head == 0, SLOPES[0], jnp.where(
        row_head == 1, SLOPES[1], jnp.where(row_head == 2, SLOPES[2], SLOPES[3]))).astype(f32) * LOG2E

    def online_update(s_list, v_list, off):
        m = dm_ref[...]
        mt = functools.reduce(jnp.maximum, [jnp.max(s, axis=-1, keepdims=True) for s in s_list])
        m_new = jnp.maximum(m, mt - off)
        shift = m_new + off
        corr = jnp.exp2(m - m_new)
        lsum = jnp.zeros_like(m)
        pv = jnp.zeros(dacc_ref.shape, f32)
        for s, vb in zip(s_list, v_list):
            p = jnp.exp2(s - shift)
            lsum = lsum + jnp.sum(p, axis=-1, keepdims=True)
            pv = pv + jnp.dot(p.astype(bf16), vb, preferred_element_type=f32)
        dm_ref[...] = m_new
        dl_ref[...] = dl_ref[...] * corr + lsum
        dacc_ref[...] = dacc_ref[...] * corr + pv

    qd = qd_ref[...]
    key = lax.broadcasted_iota(jnp.int32, (n_rows, n_keys), 1)
    key_head = key % N_HEADS
    key_tok = key // N_HEADS
    rel = (row_tok - key_tok).astype(f32)
    bias0 = jnp.where(key_head == row_head, -row_slope * rel, NEG_BIG)

    dm_ref[...] = jnp.full(dm_ref.shape, NEG_BIG, f32)
    dl_ref[...] = jnp.zeros(dl_ref.shape, f32)
    dacc_ref[...] = jnp.zeros(dacc_ref.shape, f32)

    def chunk_body(ci, carry):
        g = step * n_chunks + ci
        ahead = g + (DECODE_RING - 1)

        @pl.when(ahead < n_total)
        def _():
            for cp in chunk_copies(ahead):
                cp.start()

        for cp in chunk_copies(g):
            cp.wait()
        slot = g % DECODE_RING
        s_list, v_list = [], []
        for i in range(ppc):
            kb = kbuf[slot, i].astype(bf16)
            s = lax.dot_general(qd, kb, _NT, preferred_element_type=f32)
            s_list.append(s + (bias0 + row_slope * float(i * PAGE_SIZE)))
            v_list.append(vbuf[slot, i].astype(bf16))
        first_pos = (ci * (ppc * PAGE_SIZE)).astype(f32)
        online_update(s_list, v_list, row_slope * (float(past_len) - first_pos))
        return carry

    lax.fori_loop(0, n_chunks, chunk_body, 0)

    nk = DEC_SEQ * N_HEADS
    kcol = lax.broadcasted_iota(jnp.int32, (n_rows, nk), 1)
    kh = kcol % N_HEADS
    ktok = kcol // N_HEADS
    ok = (kh == row_head) & (ktok <= row_tok)
    bias = jnp.where(ok, -row_slope * (row_tok - ktok).astype(f32), NEG_BIG)
    s_new = lax.dot_general(qd, kn_ref[...].astype(bf16), _NT, preferred_element_type=f32)
    online_update([s_new + bias], [vn_ref[...].astype(bf16)], jnp.zeros((n_rows, 1), f32))
    o = dacc_ref[...] / dl_ref[...]
    half = n_rows // 2
    ad_ref[...] = _diff_combine(o[:half], o[half:], lam, sw_ref[...], lam_init)


def _attention(page_table, q_bf, k_bf, v_bf, lq1, lk1, lq2, lk2, subln_w, x2d, sga, c, wa, wc,
               qcat, k_new, v_new, cache_k3, cache_v3, *, batch, seq, layer_base, lam_init):
    t = ATTN_T
    assert seq % t == 0
    nq = seq // t
    dec_batch, n_pages = page_table.shape
    assert dec_batch == batch * nq
    ppc = PAGES_PER_CHUNK
    assert n_pages % ppc == 0
    n_chunks = n_pages // ppc
    assert dec_batch * n_chunks >= DECODE_RING - 1
    n_keys = PAGE_SIZE * N_HEADS
    n_rows = qcat.shape[1]
    f32, bf16 = jnp.float32, jnp.bfloat16

    tile = lambda w: pl.BlockSpec((t, w), lambda b, i, pt: (b * nq + i, 0))
    whole_seq = pl.BlockSpec((seq, ATTN_WIDTH), lambda b, i, pt: (b, 0))
    const = lambda shape: pl.BlockSpec(shape, lambda b, i, pt: (0, 0))
    per_step = lambda rows: pl.BlockSpec((None, rows, HEAD_DV), lambda b, i, pt: (b * nq + i, 0, 0))
    vec = const((1, HEAD_DQK))
    in_specs = [
        tile(ATTN_WIDTH), whole_seq, whole_seq, vec, vec, vec, vec,
        const((HEAD_DV, 1)), const((1, HEAD_DV)),
        tile(D_MODEL), tile(ATTN_WIDTH), tile(CONV_WIDTH),
        const((ATTN_WIDTH, D_MODEL)), const((CONV_WIDTH, D_MODEL)),
        per_step(n_rows), per_step(DEC_SEQ * N_HEADS), per_step(DEC_SEQ * N_HEADS),
        pl.BlockSpec(memory_space=pl.ANY), pl.BlockSpec(memory_space=pl.ANY),
    ]
    grid_spec = pltpu.PrefetchScalarGridSpec(
        num_scalar_prefetch=1,
        grid=(batch, nq),
        in_specs=in_specs,
        out_specs=[tile(D_MODEL), per_step(n_rows // 2)],
        scratch_shapes=[
            pltpu.VMEM((N_HEADS, nq, HEAD_DV + ONES_ROWS, t), bf16),
            pltpu.VMEM((N_HEADS, 2 * HEAD_DQK, 2 * t), bf16),
            pltpu.VMEM((2, N_HEADS, t, 2 * t), f32),
            pltpu.VMEM((N_HEADS, 1, 2 * t), f32),
            pltpu.VMEM((N_HEADS, HEAD_DV + ONES_ROWS, 2 * t), f32),
            pltpu.VMEM((N_HEADS, t, 2 * t), f32),
            pltpu.VMEM((N_HEADS, t, 2 * t), f32),
            pltpu.VMEM((t, ATTN_WIDTH), bf16),
            pltpu.VMEM((DECODE_RING, ppc, n_keys, HEAD_DV), f32),
            pltpu.VMEM((DECODE_RING, ppc, n_keys, HEAD_DV), f32),
            pltpu.SemaphoreType.DMA((DECODE_RING,)),
            pltpu.VMEM((n_rows, 1), f32),
            pltpu.VMEM((n_rows, 1), f32),
            pltpu.VMEM((n_rows, HEAD_DV), f32),
        ],
    )
    return pl.pallas_call(
        functools.partial(_attn_kernel, lam_init, n_chunks, n_pages * PAGE_SIZE, layer_base),
        grid_spec=grid_spec,
        out_shape=[jax.ShapeDtypeStruct((batch * seq, D_MODEL), f32),
                   jax.ShapeDtypeStruct((dec_batch, n_rows // 2, HEAD_DV), f32)],
        compiler_params=pltpu.CompilerParams(
            dimension_semantics=("arbitrary", "arbitrary"), vmem_limit_bytes=VMEM_LIMIT_BYTES),
        name="attn_fused",
    )(page_table.reshape(-1), q_bf, k_bf, v_bf, lq1, lk1, lq2, lk2,
      subln_w.reshape(HEAD_DV, 1), subln_w.reshape(1, HEAD_DV), x2d, sga, c, wa, wc,
      qcat, k_new, v_new, cache_k3, cache_v3)


def _out_kernel(x_ref, a_ref, sga_ref, c_ref, wa_ref, wc_ref, y_ref):
    ag = (a_ref[...].astype(jnp.float32) * sga_ref[...].astype(jnp.float32)).astype(jnp.bfloat16)
    y = jnp.dot(ag, wa_ref[...], preferred_element_type=jnp.float32)
    y = y + jnp.dot(c_ref[...], wc_ref[...], preferred_element_type=jnp.float32)
    y_ref[...] = x_ref[...] + y


def _out_project(x2d, a, sga, c, wa, wc):
    t_total = x2d.shape[0]
    tm = PROJ_TM
    assert t_total % tm == 0
    tok = lambda i: (i, 0)
    const = lambda i: (0, 0)
    return pl.pallas_call(
        _out_kernel,
        grid=(t_total // tm,),
        in_specs=[
            pl.BlockSpec((tm, D_MODEL), tok),
            pl.BlockSpec((tm, ATTN_WIDTH), tok),
            pl.BlockSpec((tm, ATTN_WIDTH), tok),
            pl.BlockSpec((tm, CONV_WIDTH), tok),
            pl.BlockSpec((ATTN_WIDTH, D_MODEL), const),
            pl.BlockSpec((CONV_WIDTH, D_MODEL), const),
        ],
        out_specs=pl.BlockSpec((tm, D_MODEL), tok),
        out_shape=jax.ShapeDtypeStruct((t_total, D_MODEL), jnp.float32),
        compiler_params=pltpu.CompilerParams(
            dimension_semantics=("arbitrary",), vmem_limit_bytes=VMEM_LIMIT_BYTES),
        name="out_proj",
    )(x2d, a, sga, c, wa, wc)


def _group_mean_matrix():
    g = np.kron(np.eye(MXU_TILE // HEAD_DQK, dtype=np.float32),
                np.full((HEAD_DQK, HEAD_DQK), 1.0 / HEAD_DQK, np.float32))
    return jnp.asarray(g, jnp.bfloat16)


def kernel(x_prompt, x_sample, cache_k, cache_v, state_conv, page_table, norm_w, w_in, q_norm_w, k_norm_w,
           lambda_q1, lambda_k1, lambda_q2, lambda_k2, subln_w, conv_w, w_out):
    depth = norm_w.shape[0]
    batch, seq, _ = x_prompt.shape
    dec_batch, dec_seq, _ = x_sample.shape
    assert dec_seq == DEC_SEQ
    n_pool = cache_k.shape[1]
    n_keys = PAGE_SIZE * N_HEADS
    f32 = jnp.float32
    g_mat = _group_mean_matrix()
    cache_k3 = cache_k.reshape(depth * n_pool, n_keys, HEAD_DV)
    cache_v3 = cache_v.reshape(depth * n_pool, n_keys, HEAD_DV)

    xp = x_prompt.reshape(batch * seq, D_MODEL)
    xs = x_sample.reshape(dec_batch * dec_seq, D_MODEL)
    outs = [[] for _ in range(6)]
    for layer in range(depth):
        lam_init = _lambda_init(layer)
        w_in_bf = w_in[layer].astype(jnp.bfloat16)
        w_out_bf = w_out[layer].astype(jnp.bfloat16)
        wa, wc = w_out_bf[:ATTN_WIDTH], w_out_bf[ATTN_WIDTH:]
        nw = norm_w[layer].reshape(1, D_MODEL)
        qw = jnp.tile(q_norm_w[layer], SPLIT_W // HEAD_DQK).reshape(1, SPLIT_W)
        kw = jnp.tile(k_norm_w[layer], SPLIT_W // HEAD_DQK).reshape(1, SPLIT_W)
        lvecs = [v[layer].reshape(1, HEAD_DQK).astype(f32) for v in (lambda_q1, lambda_k1, lambda_q2, lambda_k2)]
        sw = subln_w[layer].astype(f32)
        cw = conv_w[layer]

        q_bf, k, v, k_bf, v_bf, sga, c, conv_tail = _project(
            xp, nw, w_in_bf, g_mat, qw, kw, cw, batch=batch, seq=seq, decode=False)
        st = state_conv[layer].astype(f32)
        zeros = jnp.zeros((dec_batch, 1, CONV_WIDTH), f32)
        st1 = jnp.concatenate([st[:, 1:2], zeros, zeros, zeros], axis=1).reshape(-1, CONV_WIDTH)
        st2 = jnp.concatenate([st[:, 0:1], st[:, 1:2], zeros, zeros], axis=1).reshape(-1, CONV_WIDTH)
        qs_bf, ks, vs, _, _, sga_s, c_s, z = _project(
            xs, nw, w_in_bf, g_mat, qw, kw, cw, batch=dec_batch, seq=dec_seq, decode=True, states=(st1, st2))
        q5 = qs_bf.reshape(dec_batch, dec_seq, N_HEADS, 2, HEAD_DQK).transpose(0, 3, 2, 1, 4)
        eye = jnp.eye(2, dtype=qs_bf.dtype)
        qcat = (q5[:, :, :, :, None, :] * eye[None, :, None, None, :, None]).reshape(
            dec_batch, 2 * N_HEADS * dec_seq, 2 * HEAD_DQK)

        xp, a_s = _attention(
            page_table, q_bf, k_bf, v_bf, *lvecs, sw, xp, sga, c, wa, wc, qcat,
            ks.reshape(dec_batch, dec_seq * N_HEADS, HEAD_DV), vs.reshape(dec_batch, dec_seq * N_HEADS, HEAD_DV),
            cache_k3, cache_v3, batch=batch, seq=seq, layer_base=layer * n_pool, lam_init=lam_init)
        a_s = a_s.reshape(dec_batch, N_HEADS, dec_seq, HEAD_DV).transpose(0, 2, 1, 3).reshape(-1, ATTN_WIDTH)
        xs = _out_project(xs, a_s, sga_s, c_s, wa, wc)

        outs[0].append(k.reshape(batch, seq, N_HEADS, 2 * HEAD_DQK))
        outs[1].append(v.reshape(batch, seq, N_HEADS, HEAD_DV))
        outs[2].append(conv_tail)
        outs[3].append(ks.reshape(dec_batch, dec_seq, N_HEADS, 2 * HEAD_DQK))
        outs[4].append(vs.reshape(dec_batch, dec_seq, N_HEADS, HEAD_DV))
        outs[5].append(z.reshape(dec_batch, dec_seq, CONV_WIDTH)[:, dec_seq - (CONV_K - 1):])

    return (xp.reshape(batch, seq, D_MODEL), xs.reshape(dec_batch, dec_seq, D_MODEL),
            jnp.stack(outs[0]), jnp.stack(outs[1]), jnp.stack(outs[2]),
            jnp.stack(outs[3]), jnp.stack(outs[4]), jnp.stack(outs[5]))
```

```python
import functools
import math

import jax
import jax.numpy as jnp
import numpy as np
from jax import lax
from jax.experimental import pallas as pl
from jax.experimental.pallas import tpu as pltpu

D_MODEL = 1024
N_HEADS = 4
HEAD_DV = 128
HEAD_DQK = 64
ATTN_WIDTH = N_HEADS * HEAD_DV
CONV_WIDTH = 512
CONV_K = 3
PAGE_SIZE = 128
DEC_SEQ = 4
RMS_EPS = 1e-6
ATTN_SCALE = HEAD_DQK ** -0.5
LOG2E = math.log2(math.e)
N_SPLITS = 8
SPLIT_W = 512
NEG_BIG = -1e30
SLOPES = tuple(2.0 ** (-8.0 * (h + 1) / N_HEADS) for h in range(N_HEADS))

MXU_TILE = 256
VMEM_LIMIT_BYTES = 56 * 1024 * 1024
PROJ_TM = 512
ATTN_T = 256
ONES_ROWS = 16
PAGES_PER_CHUNK = 8
DECODE_RING = 5

_NT = (((1,), (1,)), ((), ()))


def _lambda_init(layer):
    return 0.8 - 0.6 * math.exp(-0.3 * layer)


def _silu(x):
    return x / (1.0 + jnp.exp(-x))


def _lambda_value(lq1, lk1, lq2, lk2, lam_init):
    s1 = jnp.sum(lq1 * lk1, axis=-1, keepdims=True)
    s2 = jnp.sum(lq2 * lk2, axis=-1, keepdims=True)
    return jnp.exp(s1) - jnp.exp(s2) + lam_init


def _proj_kernel(decode, x_ref, nw_ref, w_ref, g_ref, qw_ref, kw_ref, cw_ref, *rest):
    if decode:
        st1_ref, st2_ref = rest[:2]
        rest = rest[2:]
    q_ref, k_ref, v_ref, kb_ref, vb_ref, sga_ref, c_ref, tail_ref = rest[:8]
    zc_ref = None if decode else rest[8]

    x = x_ref[...]
    h = x * lax.rsqrt(jnp.mean(x * x, axis=-1, keepdims=True) + RMS_EPS) * nw_ref[...]
    hb = h.astype(jnp.bfloat16)

    def split(i):
        return jnp.dot(hb, w_ref[:, i * SPLIT_W:(i + 1) * SPLIT_W], preferred_element_type=jnp.float32)

    def qk_norm(y, w):
        yy = (y * y).astype(jnp.bfloat16)
        ms = jnp.concatenate(
            [jnp.dot(yy[:, i * MXU_TILE:(i + 1) * MXU_TILE], g_ref[...], preferred_element_type=jnp.float32)
             for i in range(SPLIT_W // MXU_TILE)], axis=1)
        return y * lax.rsqrt(ms + RMS_EPS) * w

    q = qk_norm(split(0), qw_ref[...])
    q_ref[...] = (q * (ATTN_SCALE * LOG2E)).astype(jnp.bfloat16)
    tm = x.shape[0]

    def store_token_head_rows(o_ref, y):
        for hd in range(N_HEADS):
            o_ref[pl.ds(hd, tm, stride=N_HEADS), :] = y[:, hd * HEAD_DV:(hd + 1) * HEAD_DV]

    k = qk_norm(split(1), kw_ref[...])
    store_token_head_rows(k_ref, k)
    kb_ref[...] = k.astype(jnp.bfloat16)
    v = split(2)
    store_token_head_rows(v_ref, v)
    vb_ref[...] = v.astype(jnp.bfloat16)
    sga_ref[...] = _silu(split(3)).astype(jnp.bfloat16)

    z = split(5) * split(4)
    row = lax.broadcasted_iota(jnp.int32, z.shape, 0)
    r1 = pltpu.roll(z, 1, 0)
    r2 = pltpu.roll(z, 2, 0)
    if decode:
        t = row % DEC_SEQ
        p1 = jnp.where(t == 0, st1_ref[...], r1)
        p2 = jnp.where(t < 2, st2_ref[...], r2)
        tail_ref[...] = z
    else:
        first = pl.program_id(1) == 0
        zc = jnp.where(first, 0.0, zc_ref[...])
        p1 = jnp.where(row == 0, zc[7:8], r1)
        p2 = jnp.where(row == 0, zc[6:7], jnp.where(row == 1, zc[7:8], r2))
        zc_ref[...] = z[tm - 8:]
        tail_ref[...] = z[tm - 2:]
    cw = cw_ref[...]
    cy = cw[2:3] * z + cw[1:2] * p1 + cw[0:1] * p2
    c_ref[...] = (split(6) * cy * _silu(split(7))).astype(jnp.bfloat16)


def _project(x2d, norm_w, w_in_bf, g_mat, qw, kw, conv_w, *, batch, seq, decode, states=None):
    t_total = x2d.shape[0]
    tm = PROJ_TM
    assert seq % tm == 0 or decode
    if decode:
        assert t_total == tm
        grid = (1, 1)
        nt = 1
    else:
        nt = seq // tm
        grid = (batch, nt)
    tok = lambda b, t: (b * nt + t, 0)
    const = lambda b, t: (0, 0)
    in_specs = [
        pl.BlockSpec((tm, D_MODEL), tok),
        pl.BlockSpec((1, D_MODEL), const),
        pl.BlockSpec((D_MODEL, N_SPLITS * SPLIT_W), const),
        pl.BlockSpec((MXU_TILE, MXU_TILE), const),
        pl.BlockSpec((1, SPLIT_W), const),
        pl.BlockSpec((1, SPLIT_W), const),
        pl.BlockSpec((CONV_K, CONV_WIDTH), const),
    ]
    args = [x2d, norm_w, w_in_bf, g_mat, qw, kw, conv_w]
    if decode:
        in_specs += [pl.BlockSpec((tm, CONV_WIDTH), tok)] * 2
        args += list(states)
    f32, bf16 = jnp.float32, jnp.bfloat16
    tok_spec = pl.BlockSpec((tm, SPLIT_W), tok)
    th_spec = pl.BlockSpec((tm * N_HEADS, HEAD_DV), tok)
    th_shape = jax.ShapeDtypeStruct((t_total * N_HEADS, HEAD_DV), f32)
    out_shape = [jax.ShapeDtypeStruct((t_total, SPLIT_W), bf16), th_shape, th_shape] + [
        jax.ShapeDtypeStruct((t_total, SPLIT_W), bf16)] * 4
    out_specs = [tok_spec, th_spec, th_spec] + [tok_spec] * 4
    if decode:
        out_shape.append(jax.ShapeDtypeStruct((t_total, CONV_WIDTH), f32))
        out_specs.append(tok_spec)
        scratch = []
    else:
        out_shape.append(jax.ShapeDtypeStruct((batch, CONV_K - 1, CONV_WIDTH), f32))
        out_specs.append(pl.BlockSpec((None, CONV_K - 1, CONV_WIDTH), lambda b, t: (b, 0, 0)))
        scratch = [pltpu.VMEM((8, CONV_WIDTH), f32)]
    return pl.pallas_call(
        functools.partial(_proj_kernel, decode),
        grid=grid,
        in_specs=in_specs,
        out_specs=out_specs,
        out_shape=out_shape,
        scratch_shapes=scratch,
        compiler_params=pltpu.CompilerParams(
            dimension_semantics=("arbitrary", "arbitrary"), vmem_limit_bytes=VMEM_LIMIT_BYTES),
        name="proj_decode" if decode else "proj_prompt",
    )(*args)


def _diff_combine(o1, o2, lam, subln_w, lam_init):
    d = o1 - lam * o2
    d = d * lax.rsqrt(jnp.mean(d * d, axis=-1, keepdims=True) + RMS_EPS)
    return d * subln_w * (1.0 - lam_init)


def _attn_kernel(lam_init, n_chunks, past_len, layer_base,
                 pt_ref, q_ref, k_ref, v_ref, lq1_ref, lk1_ref, lq2_ref, lk2_ref, swc_ref, sw_ref,
                 x_ref, sga_ref, c_ref, wa_ref, wc_ref, qd_ref, kn_ref, vn_ref, ck_hbm, cv_hbm,
                 y_ref, ad_ref,
                 vt_ref, q2t_ref, sd_ref, m_ref, acc_ref, s0_ref, s1_ref, ag_ref,
                 kbuf, vbuf, sem, dm_ref, dl_ref, dacc_ref):
    b = pl.program_id(0)
    qi = pl.program_id(1)
    n_q = pl.num_programs(1)
    step = b * n_q + qi
    n_total = pl.num_programs(0) * n_q * n_chunks
    t = ATTN_T
    n_kt = vt_ref.shape[1]
    ppc = PAGES_PER_CHUNK
    f32, bf16 = jnp.float32, jnp.bfloat16

    def chunk_copies(g):
        slot = g % DECODE_RING
        copies = []
        for i in range(ppc):
            page = layer_base + pt_ref[g * ppc + i]
            copies.append(pltpu.make_async_copy(ck_hbm.at[page], kbuf.at[slot, i], sem.at[slot]))
            copies.append(pltpu.make_async_copy(cv_hbm.at[page], vbuf.at[slot, i], sem.at[slot]))
        return copies

    def start_chunk(g):
        @pl.when(g < n_total)
        def _():
            for cp in chunk_copies(g):
                cp.start()

    @pl.when(step == 0)
    def _():
        for g in range(DECODE_RING - 2):
            for cp in chunk_copies(g):
                cp.start()

    @pl.when(step == 0)
    def _():
        r = lax.broadcasted_iota(jnp.int32, (t, 2 * t), 0)
        c = lax.broadcasted_iota(jnp.int32, (t, 2 * t), 1) % t
        rel = (c - r).astype(f32)
        for h in range(N_HEADS):
            sd = (SLOPES[h] * LOG2E) * rel
            sd_ref[0, h] = sd
            sd_ref[1, h] = jnp.where(rel >= 0.0, sd, -NEG_BIG)
            vt_ref[h, :, HEAD_DV:, :] = jnp.ones((n_kt, ONES_ROWS, t), bf16)

    @pl.when(qi == 0)
    def _():
        for h in range(N_HEADS):
            for c in range(n_kt):
                blk = v_ref[c * t:(c + 1) * t, h * HEAD_DV:(h + 1) * HEAD_DV].astype(f32)
                vt_ref[h, c, :HEAD_DV, :] = blk.T.astype(bf16)

    drow = lax.broadcasted_iota(jnp.int32, (2 * HEAD_DQK, t), 0)
    for h in range(N_HEADS):
        qt = q_ref[:, h * HEAD_DV:(h + 1) * HEAD_DV].astype(f32).T
        q2t_ref[h] = jnp.concatenate(
            [jnp.where(drow < HEAD_DQK, qt, 0.0), jnp.where(drow >= HEAD_DQK, qt, 0.0)], axis=1).astype(bf16)
        m_ref[h] = jnp.full(m_ref.shape[1:], NEG_BIG, f32)
        acc_ref[h] = jnp.zeros(acc_ref.shape[1:], f32)

    def scores(j, s_ref):
        koff = pl.multiple_of(j * t, t)
        diag = (j == qi).astype(jnp.int32)
        for h in range(N_HEADS):
            kt = k_ref[pl.ds(koff, t), h * HEAD_DV:(h + 1) * HEAD_DV]
            s_ref[h] = jnp.dot(kt, q2t_ref[h], preferred_element_type=f32) - sd_ref[diag, h]

    def accumulate(j, s_ref):
        dist_tiles = (qi - j).astype(f32)
        for h in range(N_HEADS):
            s = s_ref[h]
            off = (SLOPES[h] * LOG2E * t) * dist_tiles
            m = m_ref[h]
            m_new = jnp.maximum(m, jnp.max(s, axis=0, keepdims=True) - off)
            p = jnp.exp2(s - (m_new + off))
            corr = jnp.exp2(m - m_new)
            pv = jnp.dot(vt_ref[h, j], p.astype(bf16), preferred_element_type=f32)
            acc_ref[h] = acc_ref[h] * corr + pv
            m_ref[h] = m_new

    n_rows = 2 * N_HEADS * DEC_SEQ
    n_keys = PAGE_SIZE * N_HEADS
    row = lax.broadcasted_iota(jnp.int32, (n_rows, 1), 0)
    row_head = (row // DEC_SEQ) % N_HEADS
    row_tok = row % DEC_SEQ
    row_slope = jnp.where(row_head == 0, SLOPES[0], jnp.where(
        row_head == 1, SLOPES[1], jnp.where(row_head == 2, SLOPES[2], SLOPES[3]))).astype(f32) * LOG2E

    def online_update(s_list, v_list, off):
        m = dm_ref[...]
        mt = functools.reduce(jnp.maximum, [jnp.max(s, axis=-1, keepdims=True) for s in s_list])
        m_new = jnp.maximum(m, mt - off)
        shift = m_new + off
        corr = jnp.exp2(m - m_new)
        lsum = jnp.zeros_like(m)
        pv = jnp.zeros(dacc_ref.shape, f32)
        for s, vb in zip(s_list, v_list):
            p = jnp.exp2(s - shift)
            lsum = lsum + jnp.sum(p, axis=-1, keepdims=True)
            pv = pv + jnp.dot(p.astype(bf16), vb, preferred_element_type=f32)
        dm_ref[...] = m_new
        dl_ref[...] = dl_ref[...] * corr + lsum
        dacc_ref[...] = dacc_ref[...] * corr + pv

    qd = qd_ref[...]
    key = lax.broadcasted_iota(jnp.int32, (n_rows, n_keys), 1)
    key_head = key % N_HEADS
    key_tok = key // N_HEADS
    rel = (row_tok - key_tok).astype(f32)
    bias0 = jnp.where(key_head == row_head, -row_slope * rel, NEG_BIG)

    dm_ref[...] = jnp.full(dm_ref.shape, NEG_BIG, f32)
    dl_ref[...] = jnp.zeros(dl_ref.shape, f32)
    dacc_ref[...] = jnp.zeros(dacc_ref.shape, f32)

    nk = DEC_SEQ * N_HEADS
    kcol = lax.broadcasted_iota(jnp.int32, (n_rows, nk), 1)
    kh = kcol % N_HEADS
    ktok = kcol // N_HEADS
    ok = (kh == row_head) & (ktok <= row_tok)
    bias_new = jnp.where(ok, -row_slope * (row_tok - ktok).astype(f32), NEG_BIG)
    s_new = lax.dot_general(qd, kn_ref[...].astype(bf16), _NT, preferred_element_type=f32)
    online_update([s_new + bias_new], [vn_ref[...].astype(bf16)], jnp.zeros((n_rows, 1), f32))

    first_chunk = step * n_chunks

    def start_pair(ci):
        start_chunk(first_chunk + ci + (DECODE_RING - 2))
        start_chunk(first_chunk + ci + (DECODE_RING - 1))

    def wait_chunk(ci):
        for cp in chunk_copies(first_chunk + ci):
            cp.wait()

    def consume_chunk(ci):
        slot = (first_chunk + ci) % DECODE_RING
        s_list, v_list = [], []
        for i in range(ppc):
            kb = kbuf[slot, i].astype(bf16)
            s = lax.dot_general(qd, kb, _NT, preferred_element_type=f32)
            s_list.append(s + (bias0 + row_slope * float(i * PAGE_SIZE)))
            v_list.append(vbuf[slot, i].astype(bf16))
        first_pos = lax.convert_element_type(ci * (ppc * PAGE_SIZE), f32)
        online_update(s_list, v_list, row_slope * (float(past_len) - first_pos))

    scores(0, s0_ref)
    n_pairs = (qi + 1) // 2

    def pair(i, carry):
        j = 2 * i
        start_pair(j)
        wait_chunk(j)
        wait_chunk(j + 1)
        scores(j + 1, s1_ref)
        accumulate(j, s0_ref)
        consume_chunk(j)
        scores(jnp.minimum(j + 2, qi), s0_ref)
        accumulate(j + 1, s1_ref)
        consume_chunk(j + 1)
        return carry

    lax.fori_loop(0, n_pairs, pair, 0)

    @pl.when(qi % 2 == 0)
    def _():
        accumulate(qi, s0_ref)

    def decode_pair(i, carry):
        j = 2 * i
        start_pair(j)
        wait_chunk(j)
        wait_chunk(j + 1)
        consume_chunk(j)
        consume_chunk(j + 1)
        return carry

    lax.fori_loop(n_pairs, n_chunks // 2, decode_pair, 0)

    lam = _lambda_value(lq1_ref[...], lk1_ref[...], lq2_ref[...], lk2_ref[...], lam_init)
    for h in range(N_HEADS):
        acc = acc_ref[h]
        o = acc[:HEAD_DV] / acc[HEAD_DV:HEAD_DV + 1]
        d = o[:, :t] - lam * o[:, t:]
        d = d * lax.rsqrt(jnp.mean(d * d, axis=0, keepdims=True) + RMS_EPS)
        d = d * swc_ref[...] * (1.0 - lam_init)
        hs = slice(h * HEAD_DV, (h + 1) * HEAD_DV)
        ag_ref[:, hs] = (d.T * sga_ref[:, hs].astype(f32)).astype(bf16)

    y = jnp.dot(ag_ref[...], wa_ref[...], preferred_element_type=f32)
    y = y + jnp.dot(c_ref[...], wc_ref[...], preferred_element_type=f32)
    y_ref[...] = x_ref[...] + y

    od = dacc_ref[...] / dl_ref[...]
    half = n_rows // 2
    ad_ref[...] = _diff_combine(od[:half], od[half:], lam, sw_ref[...], lam_init)


def _attention(page_table, q_bf, k_bf, v_bf, lq1, lk1, lq2, lk2, subln_w, x2d, sga, c, wa, wc,
               qcat, k_new, v_new, cache_k3, cache_v3, *, batch, seq, layer_base, lam_init):
    t = ATTN_T
    assert seq % t == 0
    nq = seq // t
    dec_batch, n_pages = page_table.shape
    assert dec_batch == batch * nq
    ppc = PAGES_PER_CHUNK
    assert n_pages % ppc == 0
    n_chunks = n_pages // ppc
    assert n_chunks % 2 == 0 and nq // 2 <= n_chunks // 2
    assert dec_batch * n_chunks >= DECODE_RING - 2
    n_keys = PAGE_SIZE * N_HEADS
    n_rows = qcat.shape[1]
    f32, bf16 = jnp.float32, jnp.bfloat16

    tile = lambda w: pl.BlockSpec((t, w), lambda b, i, pt: (b * nq + i, 0))
    whole_seq = pl.BlockSpec((seq, ATTN_WIDTH), lambda b, i, pt: (b, 0))
    const = lambda shape: pl.BlockSpec(shape, lambda b, i, pt: (0, 0))
    per_step = lambda rows: pl.BlockSpec((None, rows, HEAD_DV), lambda b, i, pt: (b * nq + i, 0, 0))
    vec = const((1, HEAD_DQK))
    in_specs = [
        tile(ATTN_WIDTH), whole_seq, whole_seq, vec, vec, vec, vec,
        const((HEAD_DV, 1)), const((1, HEAD_DV)),
        tile(D_MODEL), tile(ATTN_WIDTH), tile(CONV_WIDTH),
        const((ATTN_WIDTH, D_MODEL)), const((CONV_WIDTH, D_MODEL)),
        per_step(n_rows), per_step(DEC_SEQ * N_HEADS), per_step(DEC_SEQ * N_HEADS),
        pl.BlockSpec(memory_space=pl.ANY), pl.BlockSpec(memory_space=pl.ANY),
    ]
    grid_spec = pltpu.PrefetchScalarGridSpec(
        num_scalar_prefetch=1,
        grid=(batch, nq),
        in_specs=in_specs,
        out_specs=[tile(D_MODEL), per_step(n_rows // 2)],
        scratch_shapes=[
            pltpu.VMEM((N_HEADS, nq, HEAD_DV + ONES_ROWS, t), bf16),
            pltpu.VMEM((N_HEADS, 2 * HEAD_DQK, 2 * t), bf16),
            pltpu.VMEM((2, N_HEADS, t, 2 * t), f32),
            pltpu.VMEM((N_HEADS, 1, 2 * t), f32),
            pltpu.VMEM((N_HEADS, HEAD_DV + ONES_ROWS, 2 * t), f32),
            pltpu.VMEM((N_HEADS, t, 2 * t), f32),
            pltpu.VMEM((N_HEADS, t, 2 * t), f32),
            pltpu.VMEM((t, ATTN_WIDTH), bf16),
            pltpu.VMEM((DECODE_RING, ppc, n_keys, HEAD_DV), f32),
            pltpu.VMEM((DECODE_RING, ppc, n_keys, HEAD_DV), f32),
            pltpu.SemaphoreType.DMA((DECODE_RING,)),
            pltpu.VMEM((n_rows, 1), f32),
            pltpu.VMEM((n_rows, 1), f32),
            pltpu.VMEM((n_rows, HEAD_DV), f32),
        ],
    )
    return pl.pallas_call(
        functools.partial(_attn_kernel, lam_init, n_chunks, n_pages * PAGE_SIZE, layer_base),
        grid_spec=grid_spec,
        out_shape=[jax.ShapeDtypeStruct((batch * seq, D_MODEL), f32),
                   jax.ShapeDtypeStruct((dec_batch, n_rows // 2, HEAD_DV), f32)],
        compiler_params=pltpu.CompilerParams(
            dimension_semantics=("arbitrary", "arbitrary"), vmem_limit_bytes=VMEM_LIMIT_BYTES),
        name="attn_fused",
    )(page_table.reshape(-1), q_bf, k_bf, v_bf, lq1, lk1, lq2, lk2,
      subln_w.reshape(HEAD_DV, 1), subln_w.reshape(1, HEAD_DV), x2d, sga, c, wa, wc,
      qcat, k_new, v_new, cache_k3, cache_v3)


def _out_kernel(x_ref, a_ref, sga_ref, c_ref, wa_ref, wc_ref, y_ref):
    ag = (a_ref[...].astype(jnp.float32) * sga_ref[...].astype(jnp.float32)).astype(jnp.bfloat16)
    y = jnp.dot(ag, wa_ref[...], preferred_element_type=jnp.float32)
    y = y + jnp.dot(c_ref[...], wc_ref[...], preferred_element_type=jnp.float32)
    y_ref[...] = x_ref[...] + y


def _out_project(x2d, a, sga, c, wa, wc):
    t_total = x2d.shape[0]
    tm = PROJ_TM
    assert t_total % tm == 0
    tok = lambda i: (i, 0)
    const = lambda i: (0, 0)
    return pl.pallas_call(
        _out_kernel,
        grid=(t_total // tm,),
        in_specs=[
            pl.BlockSpec((tm, D_MODEL), tok),
            pl.BlockSpec((tm, ATTN_WIDTH), tok),
            pl.BlockSpec((tm, ATTN_WIDTH), tok),
            pl.BlockSpec((tm, CONV_WIDTH), tok),
            pl.BlockSpec((ATTN_WIDTH, D_MODEL), const),
            pl.BlockSpec((CONV_WIDTH, D_MODEL), const),
        ],
        out_specs=pl.BlockSpec((tm, D_MODEL), tok),
        out_shape=jax.ShapeDtypeStruct((t_total, D_MODEL), jnp.float32),
        compiler_params=pltpu.CompilerParams(
            dimension_semantics=("arbitrary",), vmem_limit_bytes=VMEM_LIMIT_BYTES),
        name="out_proj",
    )(x2d, a, sga, c, wa, wc)


def _group_mean_matrix():
    g = np.kron(np.eye(MXU_TILE // HEAD_DQK, dtype=np.float32),
                np.full((HEAD_DQK, HEAD_DQK), 1.0 / HEAD_DQK, np.float32))
    return jnp.asarray(g, jnp.bfloat16)


def kernel(x_prompt, x_sample, cache_k, cache_v, state_conv, page_table, norm_w, w_in, q_norm_w, k_norm_w,
           lambda_q1, lambda_k1, lambda_q2, lambda_k2, subln_w, conv_w, w_out):
    depth = norm_w.shape[0]
    batch, seq, _ = x_prompt.shape
    dec_batch, dec_seq, _ = x_sample.shape
    assert dec_seq == DEC_SEQ
    n_pool = cache_k.shape[1]
    n_keys = PAGE_SIZE * N_HEADS
    f32 = jnp.float32
    g_mat = _group_mean_matrix()
    cache_k3 = cache_k.reshape(depth * n_pool, n_keys, HEAD_DV)
    cache_v3 = cache_v.reshape(depth * n_pool, n_keys, HEAD_DV)

    xp = x_prompt.reshape(batch * seq, D_MODEL)
    xs = x_sample.reshape(dec_batch * dec_seq, D_MODEL)
    outs = [[] for _ in range(6)]
    for layer in range(depth):
        lam_init = _lambda_init(layer)
        w_in_bf = w_in[layer].astype(jnp.bfloat16)
        w_out_bf = w_out[layer].astype(jnp.bfloat16)
        wa, wc = w_out_bf[:ATTN_WIDTH], w_out_bf[ATTN_WIDTH:]
        nw = norm_w[layer].reshape(1, D_MODEL)
        qw = jnp.tile(q_norm_w[layer], SPLIT_W // HEAD_DQK).reshape(1, SPLIT_W)
        kw = jnp.tile(k_norm_w[layer], SPLIT_W // HEAD_DQK).reshape(1, SPLIT_W)
        lvecs = [v[layer].reshape(1, HEAD_DQK).astype(f32) for v in (lambda_q1, lambda_k1, lambda_q2, lambda_k2)]
        sw = subln_w[layer].astype(f32)
        cw = conv_w[layer]

        q_bf, k, v, k_bf, v_bf, sga, c, conv_tail = _project(
            xp, nw, w_in_bf, g_mat, qw, kw, cw, batch=batch, seq=seq, decode=False)
        st = state_conv[layer].astype(f32)
        zeros = jnp.zeros((dec_batch, 1, CONV_WIDTH), f32)
        st1 = jnp.concatenate([st[:, 1:2], zeros, zeros, zeros], axis=1).reshape(-1, CONV_WIDTH)
        st2 = jnp.concatenate([st[:, 0:1], st[:, 1:2], zeros, zeros], axis=1).reshape(-1, CONV_WIDTH)
        qs_bf, ks, vs, _, _, sga_s, c_s, z = _project(
            xs, nw, w_in_bf, g_mat, qw, kw, cw, batch=dec_batch, seq=dec_seq, decode=True, states=(st1, st2))
        q5 = qs_bf.reshape(dec_batch, dec_seq, N_HEADS, 2, HEAD_DQK).transpose(0, 3, 2, 1, 4)
        eye = jnp.eye(2, dtype=qs_bf.dtype)
        qcat = (q5[:, :, :, :, None, :] * eye[None, :, None, None, :, None]).reshape(
            dec_batch, 2 * N_HEADS * dec_seq, 2 * HEAD_DQK)

        xp, a_s = _attention(
            page_table, q_bf, k_bf, v_bf, *lvecs, sw, xp, sga, c, wa, wc, qcat,
            ks.reshape(dec_batch, dec_seq * N_HEADS, HEAD_DV), vs.reshape(dec_batch, dec_seq * N_HEADS, HEAD_DV),
            cache_k3, cache_v3, batch=batch, seq=seq, layer_base=layer * n_pool, lam_init=lam_init)
        a_s = a_s.reshape(dec_batch, N_HEADS, dec_seq, HEAD_DV).transpose(0, 2, 1, 3).reshape(-1, ATTN_WIDTH)
        xs = _out_project(xs, a_s, sga_s, c_s, wa, wc)

        outs[0].append(k.reshape(batch, seq, N_HEADS, 2 * HEAD_DQK))
        outs[1].append(v.reshape(batch, seq, N_HEADS, HEAD_DV))
        outs[2].append(conv_tail)
        outs[3].append(ks.reshape(dec_batch, dec_seq, N_HEADS, 2 * HEAD_DQK))
        outs[4].append(vs.reshape(dec_batch, dec_seq, N_HEADS, HEAD_DV))
        outs[5].append(z.reshape(dec_batch, dec_seq, CONV_WIDTH)[:, dec_seq - (CONV_K - 1):])

    return (xp.reshape(batch, seq, D_MODEL), xs.reshape(dec_batch, dec_seq, D_MODEL),
            jnp.stack(outs[0]), jnp.stack(outs[1]), jnp.stack(outs[2]),
            jnp.stack(outs[3]), jnp.stack(outs[4]), jnp.stack(outs[5]))
```

```python
import functools
import math

import jax
import jax.numpy as jnp
import numpy as np
from jax import lax
from jax.experimental import pallas as pl
from jax.experimental.pallas import tpu as pltpu

D_MODEL = 1024
N_HEADS = 4
HEAD_DV = 128
HEAD_DQK = 64
ATTN_WIDTH = N_HEADS * HEAD_DV
CONV_WIDTH = 512
CONV_K = 3
PAGE_SIZE = 128
DEC_SEQ = 4
RMS_EPS = 1e-6
ATTN_SCALE = HEAD_DQK ** -0.5
LOG2E = math.log2(math.e)
N_SPLITS = 8
SPLIT_W = 512
NEG_BIG = -1e30
SLOPES = tuple(2.0 ** (-8.0 * (h + 1) / N_HEADS) for h in range(N_HEADS))

MXU_TILE = 256
VMEM_CAPACITY_BYTES = 64 * 1024 * 1024
VMEM_LIMIT_BYTES = VMEM_CAPACITY_BYTES - 6 * 1024 * 1024
DECODE_TM = 512
ATTN_T = 256
ONES_ROWS = 16
PAGES_PER_CHUNK = 8
DECODE_RING = 4

_NT = (((1,), (1,)), ((), ()))


def _lambda_init(layer):
    return 0.8 - 0.6 * math.exp(-0.3 * layer)


def _silu(x):
    return x / (1.0 + jnp.exp(-x))


def _lambda_value(lq1, lk1, lq2, lk2, lam_init):
    s1 = jnp.sum(lq1 * lk1, axis=-1, keepdims=True)
    s2 = jnp.sum(lq2 * lk2, axis=-1, keepdims=True)
    return jnp.exp(s1) - jnp.exp(s2) + lam_init


def _diff_combine(o1, o2, lam, subln_w, lam_init):
    d = o1 - lam * o2
    d = d * lax.rsqrt(jnp.mean(d * d, axis=-1, keepdims=True) + RMS_EPS)
    return d * subln_w * (1.0 - lam_init)


def _normed_splits(x, nw, w_ref):
    h = x * lax.rsqrt(jnp.mean(x * x, axis=-1, keepdims=True) + RMS_EPS) * nw
    hb = h.astype(jnp.bfloat16)

    def split(i):
        return jnp.dot(hb, w_ref[:, i * SPLIT_W:(i + 1) * SPLIT_W], preferred_element_type=jnp.float32)

    return split


def _qk_norm(y, w, g_ref):
    yy = (y * y).astype(jnp.bfloat16)
    ms = jnp.concatenate(
        [jnp.dot(yy[:, i * MXU_TILE:(i + 1) * MXU_TILE], g_ref[...], preferred_element_type=jnp.float32)
         for i in range(SPLIT_W // MXU_TILE)], axis=1)
    return y * lax.rsqrt(ms + RMS_EPS) * w


def _store_token_head_rows(o_ref, y):
    for hd in range(N_HEADS):
        o_ref[pl.ds(hd, y.shape[0], stride=N_HEADS), :] = y[:, hd * HEAD_DV:(hd + 1) * HEAD_DV]


def _gated_conv(z, p1, p2, cw, b_gate, gate_c):
    cy = cw[2:3] * z + cw[1:2] * p1 + cw[0:1] * p2
    return (b_gate * cy * _silu(gate_c)).astype(jnp.bfloat16)


def _proj_decode_kernel(x_ref, nw_ref, w_ref, g_ref, qw_ref, kw_ref, cw_ref, st1_ref, st2_ref,
                        q_ref, k_ref, v_ref, sga_ref, c_ref, z_ref):
    split = _normed_splits(x_ref[...], nw_ref[...], w_ref)
    q = _qk_norm(split(0), qw_ref[...], g_ref)
    q_ref[...] = (q * (ATTN_SCALE * LOG2E)).astype(jnp.bfloat16)
    _store_token_head_rows(k_ref, _qk_norm(split(1), kw_ref[...], g_ref))
    _store_token_head_rows(v_ref, split(2))
    sga_ref[...] = _silu(split(3)).astype(jnp.bfloat16)
    z = split(5) * split(4)
    z_ref[...] = z
    tok = lax.broadcasted_iota(jnp.int32, z.shape, 0) % DEC_SEQ
    p1 = jnp.where(tok == 0, st1_ref[...], pltpu.roll(z, 1, 0))
    p2 = jnp.where(tok < 2, st2_ref[...], pltpu.roll(z, 2, 0))
    c_ref[...] = _gated_conv(z, p1, p2, cw_ref[...], split(6), split(7))


def _project_decode(x2d, norm_w, w_in_bf, g_mat, qw, kw, conv_w, st1, st2):
    t_total = x2d.shape[0]
    tm = DECODE_TM
    assert t_total % tm == 0 and tm % DEC_SEQ == 0
    tok = lambda i: (i, 0)
    const = lambda i: (0, 0)
    f32, bf16 = jnp.float32, jnp.bfloat16
    tok_spec = pl.BlockSpec((tm, SPLIT_W), tok)
    th_spec = pl.BlockSpec((tm * N_HEADS, HEAD_DV), tok)
    th_shape = jax.ShapeDtypeStruct((t_total * N_HEADS, HEAD_DV), f32)
    wide = lambda d: jax.ShapeDtypeStruct((t_total, SPLIT_W), d)
    return pl.pallas_call(
        _proj_decode_kernel,
        grid=(t_total // tm,),
        in_specs=[
            pl.BlockSpec((tm, D_MODEL), tok),
            pl.BlockSpec((1, D_MODEL), const),
            pl.BlockSpec((D_MODEL, N_SPLITS * SPLIT_W), const),
            pl.BlockSpec((MXU_TILE, MXU_TILE), const),
            pl.BlockSpec((1, SPLIT_W), const),
            pl.BlockSpec((1, SPLIT_W), const),
            pl.BlockSpec((CONV_K, CONV_WIDTH), const),
            tok_spec, tok_spec,
        ],
        out_specs=[tok_spec, th_spec, th_spec, tok_spec, tok_spec, tok_spec],
        out_shape=[wide(bf16), th_shape, th_shape, wide(bf16), wide(bf16), wide(f32)],
        compiler_params=pltpu.CompilerParams(
            dimension_semantics=("arbitrary",), vmem_limit_bytes=VMEM_LIMIT_BYTES),
        name="proj_decode",
    )(x2d, norm_w, w_in_bf, g_mat, qw, kw, conv_w, st1, st2)


def _fused_kernel(lam_init, n_chunks, past_len, layer_base,
                  pt_ref, x_ref, nw_ref, w_ref, g_ref, qw_ref, kw_ref, cw_ref,
                  lq1_ref, lk1_ref, lq2_ref, lk2_ref, swc_ref, sw_ref, wa_ref, wc_ref,
                  qd_ref, kn_ref, vn_ref, ck_hbm, cv_hbm,
                  y_ref, k_ref, v_ref, tail_ref, ad_ref,
                  kb_ref, vt_ref, q2t_ref, sd_ref, m_ref, acc_ref, s0_ref, s1_ref,
                  ag_ref, sga_ref, c_ref, zc_ref,
                  kbuf, vbuf, sem, dm_ref, dl_ref, dacc_ref):
    b = pl.program_id(0)
    qi = pl.program_id(1)
    n_q = pl.num_programs(1)
    step = b * n_q + qi
    n_total = pl.num_programs(0) * n_q * n_chunks
    t = ATTN_T
    n_kt = vt_ref.shape[1]
    ppc = PAGES_PER_CHUNK
    f32, bf16 = jnp.float32, jnp.bfloat16

    def chunk_copies(g):
        slot = g % DECODE_RING
        copies = []
        for i in range(ppc):
            page = layer_base + pt_ref[g * ppc + i]
            copies.append(pltpu.make_async_copy(ck_hbm.at[page], kbuf.at[slot, i], sem.at[slot]))
            copies.append(pltpu.make_async_copy(cv_hbm.at[page], vbuf.at[slot, i], sem.at[slot]))
        return copies

    def start_chunk(g):
        @pl.when(g < n_total)
        def _():
            for cp in chunk_copies(g):
                cp.start()

    @pl.when(step == 0)
    def _():
        for g in range(DECODE_RING - 2):
            for cp in chunk_copies(g):
                cp.start()

    @pl.when(step == 0)
    def _():
        r = lax.broadcasted_iota(jnp.int32, (t, 2 * t), 0)
        c = lax.broadcasted_iota(jnp.int32, (t, 2 * t), 1) % t
        rel = (c - r).astype(f32)
        for h in range(N_HEADS):
            sd = (SLOPES[h] * LOG2E) * rel
            sd_ref[0, h] = sd
            sd_ref[1, h] = jnp.where(rel >= 0.0, sd, -NEG_BIG)
            vt_ref[h, :, HEAD_DV:, :] = jnp.ones((n_kt, ONES_ROWS, t), bf16)

    x = x_ref[...]
    split = _normed_splits(x, nw_ref[...], w_ref)
    q = _qk_norm(split(0), qw_ref[...], g_ref) * (ATTN_SCALE * LOG2E)
    drow = lax.broadcasted_iota(jnp.int32, (2 * HEAD_DQK, t), 0)
    for h in range(N_HEADS):
        qt = q[:, h * HEAD_DV:(h + 1) * HEAD_DV].T
        q2t_ref[h] = jnp.concatenate(
            [jnp.where(drow < HEAD_DQK, qt, 0.0), jnp.where(drow >= HEAD_DQK, qt, 0.0)], axis=1).astype(bf16)
        m_ref[h] = jnp.full(m_ref.shape[1:], NEG_BIG, f32)
        acc_ref[h] = jnp.zeros(acc_ref.shape[1:], f32)
    k = _qk_norm(split(1), kw_ref[...], g_ref)
    _store_token_head_rows(k_ref, k)
    kb_ref[pl.ds(pl.multiple_of(qi * t, t), t), :] = k.astype(bf16)
    v = split(2)
    _store_token_head_rows(v_ref, v)
    for h in range(N_HEADS):
        vt_ref[h, qi, :HEAD_DV, :] = v[:, h * HEAD_DV:(h + 1) * HEAD_DV].T.astype(bf16)
    sga_ref[...] = _silu(split(3)).astype(bf16)

    z = split(5) * split(4)
    row = lax.broadcasted_iota(jnp.int32, z.shape, 0)
    zc = jnp.where(qi == 0, 0.0, zc_ref[...])
    p1 = jnp.where(row == 0, zc[7:8], pltpu.roll(z, 1, 0))
    p2 = jnp.where(row == 0, zc[6:7], jnp.where(row == 1, zc[7:8], pltpu.roll(z, 2, 0)))
    zc_ref[...] = z[t - 8:]
    tail_ref[...] = z[t - (CONV_K - 1):]
    c_ref[...] = _gated_conv(z, p1, p2, cw_ref[...], split(6), split(7))

    def scores(j, s_ref):
        koff = pl.multiple_of(j * t, t)
        diag = (j == qi).astype(jnp.int32)
        for h in range(N_HEADS):
            kt = kb_ref[pl.ds(koff, t), h * HEAD_DV:(h + 1) * HEAD_DV]
            s_ref[h] = jnp.dot(kt, q2t_ref[h], preferred_element_type=f32) - sd_ref[diag, h]

    def accumulate(j, s_ref):
        dist_tiles = (qi - j).astype(f32)
        for h in range(N_HEADS):
            s = s_ref[h]
            off = (SLOPES[h] * LOG2E * t) * dist_tiles
            m = m_ref[h]
            m_new = jnp.maximum(m, jnp.max(s, axis=0, keepdims=True) - off)
            p = jnp.exp2(s - (m_new + off))
            corr = jnp.exp2(m - m_new)
            pv = jnp.dot(vt_ref[h, j], p.astype(bf16), preferred_element_type=f32)
            acc_ref[h] = acc_ref[h] * corr + pv
            m_ref[h] = m_new

    n_rows = 2 * N_HEADS * DEC_SEQ
    n_keys = PAGE_SIZE * N_HEADS
    drw = lax.broadcasted_iota(jnp.int32, (n_rows, 1), 0)
    row_head = (drw // DEC_SEQ) % N_HEADS
    row_tok = drw % DEC_SEQ
    row_slope = jnp.where(row_head == 0, SLOPES[0], jnp.where(
        row_head == 1, SLOPES[1], jnp.where(row_head == 2, SLOPES[2], SLOPES[3]))).astype(f32) * LOG2E

    def online_update(s_list, v_list, off):
        m = dm_ref[...]
        mt = functools.reduce(jnp.maximum, [jnp.max(s, axis=-1, keepdims=True) for s in s_list])
        m_new = jnp.maximum(m, mt - off)
        shift = m_new + off
        corr = jnp.exp2(m - m_new)
        lsum = jnp.zeros_like(m)
        pv = jnp.zeros(dacc_ref.shape, f32)
        for s, vb in zip(s_list, v_list):
            p = jnp.exp2(s - shift)
            lsum = lsum + jnp.sum(p, axis=-1, keepdims=True)
            pv = pv + jnp.dot(p.astype(bf16), vb, preferred_element_type=f32)
        dm_ref[...] = m_new
        dl_ref[...] = dl_ref[...] * corr + lsum
        dacc_ref[...] = dacc_ref[...] * corr + pv

    qd = qd_ref[...]
    key = lax.broadcasted_iota(jnp.int32, (n_rows, n_keys), 1)
    key_head = key % N_HEADS
    key_tok = key // N_HEADS
    rel = (row_tok - key_tok).astype(f32)
    bias0 = jnp.where(key_head == row_head, -row_slope * rel, NEG_BIG)

    dm_ref[...] = jnp.full(dm_ref.shape, NEG_BIG, f32)
    dl_ref[...] = jnp.zeros(dl_ref.shape, f32)
    dacc_ref[...] = jnp.zeros(dacc_ref.shape, f32)

    nk = DEC_SEQ * N_HEADS
    kcol = lax.broadcasted_iota(jnp.int32, (n_rows, nk), 1)
    kh = kcol % N_HEADS
    ktok = kcol // N_HEADS
    ok = (kh == row_head) & (ktok <= row_tok)
    bias_new = jnp.where(ok, -row_slope * (row_tok - ktok).astype(f32), NEG_BIG)
    s_new = lax.dot_general(qd, kn_ref[...].astype(bf16), _NT, preferred_element_type=f32)
    online_update([s_new + bias_new], [vn_ref[...].astype(bf16)], jnp.zeros((n_rows, 1), f32))

    first_chunk = step * n_chunks

    def start_pair(ci):
        start_chunk(first_chunk + ci + (DECODE_RING - 2))
        start_chunk(first_chunk + ci + (DECODE_RING - 1))

    def wait_chunk(ci):
        for cp in chunk_copies(first_chunk + ci):
            cp.wait()

    def consume_chunk(ci):
        slot = (first_chunk + ci) % DECODE_RING
        s_list, v_list = [], []
        for i in range(ppc):
            kb = kbuf[slot, i].astype(bf16)
            s = lax.dot_general(qd, kb, _NT, preferred_element_type=f32)
            s_list.append(s + (bias0 + row_slope * float(i * PAGE_SIZE)))
            v_list.append(vbuf[slot, i].astype(bf16))
        first_pos = lax.convert_element_type(ci * (ppc * PAGE_SIZE), f32)
        online_update(s_list, v_list, row_slope * (float(past_len) - first_pos))

    scores(0, s0_ref)
    n_pairs = (qi + 1) // 2

    def pair(i, carry):
        j = 2 * i
        start_pair(j)
        wait_chunk(j)
        wait_chunk(j + 1)
        scores(j + 1, s1_ref)
        accumulate(j, s0_ref)
        consume_chunk(j)
        scores(jnp.minimum(j + 2, qi), s0_ref)
        accumulate(j + 1, s1_ref)
        consume_chunk(j + 1)
        return carry

    lax.fori_loop(0, n_pairs, pair, 0)

    @pl.when(qi % 2 == 0)
    def _():
        accumulate(qi, s0_ref)

    def decode_pair(i, carry):
        j = 2 * i
        start_pair(j)
        wait_chunk(j)
        wait_chunk(j + 1)
        consume_chunk(j)
        consume_chunk(j + 1)
        return carry

    lax.fori_loop(n_pairs, n_chunks // 2, decode_pair, 0)

    lam = _lambda_value(lq1_ref[...], lk1_ref[...], lq2_ref[...], lk2_ref[...], lam_init)
    for h in range(N_HEADS):
        acc = acc_ref[h]
        o = acc[:HEAD_DV] / acc[HEAD_DV:HEAD_DV + 1]
        d = o[:, :t] - lam * o[:, t:]
        d = d * lax.rsqrt(jnp.mean(d * d, axis=0, keepdims=True) + RMS_EPS)
        d = d * swc_ref[...] * (1.0 - lam_init)
        hs = slice(h * HEAD_DV, (h + 1) * HEAD_DV)
        ag_ref[:, hs] = (d.T * sga_ref[:, hs].astype(f32)).astype(bf16)

    y = jnp.dot(ag_ref[...], wa_ref[...], preferred_element_type=f32)
    y = y + jnp.dot(c_ref[...], wc_ref[...], preferred_element_type=f32)
    y_ref[...] = x_ref[...] + y

    od = dacc_ref[...] / dl_ref[...]
    half = n_rows // 2
    ad_ref[...] = _diff_combine(od[:half], od[half:], lam, sw_ref[...], lam_init)


def _fused_layer(page_table, x2d, norm_w, w_in_bf, g_mat, qw, kw, conv_w, lq1, lk1, lq2, lk2, subln_w, wa, wc,
                 qcat, k_new, v_new, cache_k3, cache_v3, *, batch, seq, layer_base, lam_init):
    t = ATTN_T
    assert seq % t == 0
    nq = seq // t
    dec_batch, n_pages = page_table.shape
    assert dec_batch == batch * nq
    ppc = PAGES_PER_CHUNK
    assert n_pages % ppc == 0
    n_chunks = n_pages // ppc
    assert n_chunks % 2 == 0 and nq // 2 <= n_chunks // 2
    assert dec_batch * n_chunks >= DECODE_RING - 2
    n_keys = PAGE_SIZE * N_HEADS
    n_rows = qcat.shape[1]
    f32, bf16 = jnp.float32, jnp.bfloat16

    tile = lambda w: pl.BlockSpec((t, w), lambda b, i, pt: (b * nq + i, 0))
    th_tile = pl.BlockSpec((t * N_HEADS, HEAD_DV), lambda b, i, pt: (b * nq + i, 0))
    const = lambda shape: pl.BlockSpec(shape, lambda b, i, pt: (0, 0))
    per_step = lambda rows: pl.BlockSpec((None, rows, HEAD_DV), lambda b, i, pt: (b * nq + i, 0, 0))
    vec = const((1, HEAD_DQK))
    in_specs = [
        tile(D_MODEL), const((1, D_MODEL)), const((D_MODEL, N_SPLITS * SPLIT_W)), const((MXU_TILE, MXU_TILE)),
        const((1, SPLIT_W)), const((1, SPLIT_W)), const((CONV_K, CONV_WIDTH)),
        vec, vec, vec, vec, const((HEAD_DV, 1)), const((1, HEAD_DV)),
        const((ATTN_WIDTH, D_MODEL)), const((CONV_WIDTH, D_MODEL)),
        per_step(n_rows), per_step(DEC_SEQ * N_HEADS), per_step(DEC_SEQ * N_HEADS),
        pl.BlockSpec(memory_space=pl.ANY), pl.BlockSpec(memory_space=pl.ANY),
    ]
    out_specs = [
        tile(D_MODEL), th_tile, th_tile,
        pl.BlockSpec((None, CONV_K - 1, CONV_WIDTH), lambda b, i, pt: (b, 0, 0)),
        per_step(n_rows // 2),
    ]
    th_shape = jax.ShapeDtypeStruct((batch * seq * N_HEADS, HEAD_DV), f32)
    out_shape = [
        jax.ShapeDtypeStruct((batch * seq, D_MODEL), f32), th_shape, th_shape,
        jax.ShapeDtypeStruct((batch, CONV_K - 1, CONV_WIDTH), f32),
        jax.ShapeDtypeStruct((dec_batch, n_rows // 2, HEAD_DV), f32),
    ]
    grid_spec = pltpu.PrefetchScalarGridSpec(
        num_scalar_prefetch=1,
        grid=(batch, nq),
        in_specs=in_specs,
        out_specs=out_specs,
        scratch_shapes=[
            pltpu.VMEM((seq, ATTN_WIDTH), bf16),
            pltpu.VMEM((N_HEADS, nq, HEAD_DV + ONES_ROWS, t), bf16),
            pltpu.VMEM((N_HEADS, 2 * HEAD_DQK, 2 * t), bf16),
            pltpu.VMEM((2, N_HEADS, t, 2 * t), f32),
            pltpu.VMEM((N_HEADS, 1, 2 * t), f32),
            pltpu.VMEM((N_HEADS, HEAD_DV + ONES_ROWS, 2 * t), f32),
            pltpu.VMEM((N_HEADS, t, 2 * t), f32),
            pltpu.VMEM((N_HEADS, t, 2 * t), f32),
            pltpu.VMEM((t, ATTN_WIDTH), bf16),
            pltpu.VMEM((t, ATTN_WIDTH), bf16),
            pltpu.VMEM((t, CONV_WIDTH), bf16),
            pltpu.VMEM((8, CONV_WIDTH), f32),
            pltpu.VMEM((DECODE_RING, ppc, n_keys, HEAD_DV), f32),
            pltpu.VMEM((DECODE_RING, ppc, n_keys, HEAD_DV), f32),
            pltpu.SemaphoreType.DMA((DECODE_RING,)),
            pltpu.VMEM((n_rows, 1), f32),
            pltpu.VMEM((n_rows, 1), f32),
            pltpu.VMEM((n_rows, HEAD_DV), f32),
        ],
    )
    return pl.pallas_call(
        functools.partial(_fused_kernel, lam_init, n_chunks, n_pages * PAGE_SIZE, layer_base),
        grid_spec=grid_spec,
        out_shape=out_shape,
        compiler_params=pltpu.CompilerParams(
            dimension_semantics=("arbitrary", "arbitrary"), vmem_limit_bytes=VMEM_LIMIT_BYTES),
        name="fused_layer",
    )(page_table.reshape(-1), x2d, norm_w, w_in_bf, g_mat, qw, kw, conv_w, lq1, lk1, lq2, lk2,
      subln_w.reshape(HEAD_DV, 1), subln_w.reshape(1, HEAD_DV), wa, wc,
      qcat, k_new, v_new, cache_k3, cache_v3)


def _out_kernel(x_ref, a_ref, sga_ref, c_ref, wa_ref, wc_ref, y_ref):
    ag = (a_ref[...] * sga_ref[...].astype(jnp.float32)).astype(jnp.bfloat16)
    y = jnp.dot(ag, wa_ref[...], preferred_element_type=jnp.float32)
    y = y + jnp.dot(c_ref[...], wc_ref[...], preferred_element_type=jnp.float32)
    y_ref[...] = x_ref[...] + y


def _out_project(x2d, a, sga, c, wa, wc):
    t_total = x2d.shape[0]
    tm = DECODE_TM
    assert t_total % tm == 0
    tok = lambda i: (i, 0)
    const = lambda i: (0, 0)
    return pl.pallas_call(
        _out_kernel,
        grid=(t_total // tm,),
        in_specs=[
            pl.BlockSpec((tm, D_MODEL), tok),
            pl.BlockSpec((tm, ATTN_WIDTH), tok),
            pl.BlockSpec((tm, ATTN_WIDTH), tok),
            pl.BlockSpec((tm, CONV_WIDTH), tok),
            pl.BlockSpec((ATTN_WIDTH, D_MODEL), const),
            pl.BlockSpec((CONV_WIDTH, D_MODEL), const),
        ],
        out_specs=pl.BlockSpec((tm, D_MODEL), tok),
        out_shape=jax.ShapeDtypeStruct((t_total, D_MODEL), jnp.float32),
        compiler_params=pltpu.CompilerParams(
            dimension_semantics=("arbitrary",), vmem_limit_bytes=VMEM_LIMIT_BYTES),
        name="out_proj",
    )(x2d, a, sga, c, wa, wc)


def _group_mean_matrix():
    g = np.kron(np.eye(MXU_TILE // HEAD_DQK, dtype=np.float32),
                np.full((HEAD_DQK, HEAD_DQK), 1.0 / HEAD_DQK, np.float32))
    return jnp.asarray(g, jnp.bfloat16)


def kernel(x_prompt, x_sample, cache_k, cache_v, state_conv, page_table, norm_w, w_in, q_norm_w, k_norm_w,
           lambda_q1, lambda_k1, lambda_q2, lambda_k2, subln_w, conv_w, w_out):
    depth = norm_w.shape[0]
    batch, seq, _ = x_prompt.shape
    dec_batch, dec_seq, _ = x_sample.shape
    assert dec_seq == DEC_SEQ
    n_pool = cache_k.shape[1]
    n_keys = PAGE_SIZE * N_HEADS
    f32 = jnp.float32
    g_mat = _group_mean_matrix()
    cache_k3 = cache_k.reshape(depth * n_pool, n_keys, HEAD_DV)
    cache_v3 = cache_v.reshape(depth * n_pool, n_keys, HEAD_DV)

    xp = x_prompt.reshape(batch * seq, D_MODEL)
    xs = x_sample.reshape(dec_batch * dec_seq, D_MODEL)
    outs = [[] for _ in range(6)]
    for layer in range(depth):
        lam_init = _lambda_init(layer)
        w_in_bf = w_in[layer].astype(jnp.bfloat16)
        w_out_bf = w_out[layer].astype(jnp.bfloat16)
        wa, wc = w_out_bf[:ATTN_WIDTH], w_out_bf[ATTN_WIDTH:]
        nw = norm_w[layer].reshape(1, D_MODEL)
        qw = jnp.tile(q_norm_w[layer], SPLIT_W // HEAD_DQK).reshape(1, SPLIT_W)
        kw = jnp.tile(k_norm_w[layer], SPLIT_W // HEAD_DQK).reshape(1, SPLIT_W)
        lvecs = [v[layer].reshape(1, HEAD_DQK).astype(f32) for v in (lambda_q1, lambda_k1, lambda_q2, lambda_k2)]
        sw = subln_w[layer].astype(f32)
        cw = conv_w[layer]

        st = state_conv[layer].astype(f32)
        zeros = jnp.zeros((dec_batch, 1, CONV_WIDTH), f32)
        st1 = jnp.concatenate([st[:, 1:2], zeros, zeros, zeros], axis=1).reshape(-1, CONV_WIDTH)
        st2 = jnp.concatenate([st[:, 0:1], st[:, 1:2], zeros, zeros], axis=1).reshape(-1, CONV_WIDTH)
        qs_bf, ks, vs, sga_s, c_s, z = _project_decode(xs, nw, w_in_bf, g_mat, qw, kw, cw, st1, st2)
        q5 = qs_bf.reshape(dec_batch, dec_seq, N_HEADS, 2, HEAD_DQK).transpose(0, 3, 2, 1, 4)
        eye = jnp.eye(2, dtype=qs_bf.dtype)
        qcat = (q5[:, :, :, :, None, :] * eye[None, :, None, None, :, None]).reshape(
            dec_batch, 2 * N_HEADS * dec_seq, 2 * HEAD_DQK)

        xp, k, v, conv_tail, a_s = _fused_layer(
            page_table, xp, nw, w_in_bf, g_mat, qw, kw, cw, *lvecs, sw, wa, wc, qcat,
            ks.reshape(dec_batch, dec_seq * N_HEADS, HEAD_DV), vs.reshape(dec_batch, dec_seq * N_HEADS, HEAD_DV),
            cache_k3, cache_v3, batch=batch, seq=seq, layer_base=layer * n_pool, lam_init=lam_init)
        a_s = a_s.reshape(dec_batch, N_HEADS, dec_seq, HEAD_DV).transpose(0, 2, 1, 3).reshape(-1, ATTN_WIDTH)
        xs = _out_project(xs, a_s, sga_s, c_s, wa, wc)

        outs[0].append(k.reshape(batch, seq, N_HEADS, 2 * HEAD_DQK))
        outs[1].append(v.reshape(batch, seq, N_HEADS, HEAD_DV))
        outs[2].append(conv_tail)
        outs[3].append(ks.reshape(dec_batch, dec_seq, N_HEADS, 2 * HEAD_DQK))
        outs[4].append(vs.reshape(dec_batch, dec_seq, N_HEADS, HEAD_DV))
        outs[5].append(z.reshape(dec_batch, dec_seq, CONV_WIDTH)[:, dec_seq - (CONV_K - 1):])

    return (xp.reshape(batch, seq, D_MODEL), xs.reshape(dec_batch, dec_seq, D_MODEL),
            jnp.stack(outs[0]), jnp.stack(outs[1]), jnp.stack(outs[2]),
            jnp.stack(outs[3]), jnp.stack(outs[4]), jnp.stack(outs[5]))
```

```python
import functools
import math

import jax
import jax.numpy as jnp
import numpy as np
from jax import lax
from jax.experimental import pallas as pl
from jax.experimental.pallas import tpu as pltpu

D_MODEL = 1024
N_HEADS = 4
HEAD_DV = 128
HEAD_DQK = 64
ATTN_WIDTH = N_HEADS * HEAD_DV
CONV_WIDTH = 512
CONV_K = 3
PAGE_SIZE = 128
DEC_SEQ = 4
RMS_EPS = 1e-6
ATTN_SCALE = HEAD_DQK ** -0.5
LOG2E = math.log2(math.e)
N_SPLITS = 8
SPLIT_W = 512
NEG_BIG = -1e30
SLOPES = tuple(2.0 ** (-8.0 * (h + 1) / N_HEADS) for h in range(N_HEADS))

MXU_TILE = 256
VMEM_CAPACITY_BYTES = 64 * 1024 * 1024
VMEM_LIMIT_BYTES = VMEM_CAPACITY_BYTES - 4 * 1024 * 1024
PROJ_TM = 512
ATTN_T = 256
ONES_ROWS = 16
PAGES_PER_CHUNK = 8
DECODE_RING = 7

_NT = (((1,), (1,)), ((), ()))


def _lambda_init(layer):
    return 0.8 - 0.6 * math.exp(-0.3 * layer)


def _silu(x):
    return x / (1.0 + jnp.exp(-x))


def _lambda_value(lq1, lk1, lq2, lk2, lam_init):
    s1 = jnp.sum(lq1 * lk1, axis=-1, keepdims=True)
    s2 = jnp.sum(lq2 * lk2, axis=-1, keepdims=True)
    return jnp.exp(s1) - jnp.exp(s2) + lam_init


def _proj_kernel(decode, x_ref, nw_ref, w_ref, g_ref, qw_ref, kw_ref, cw_ref, *rest):
    if decode:
        st1_ref, st2_ref = rest[:2]
        rest = rest[2:]
    q_ref, k_ref, v_ref, kb_ref, vb_ref, sga_ref, c_ref, tail_ref = rest[:8]
    zc_ref = None if decode else rest[8]

    x = x_ref[...]
    h = x * lax.rsqrt(jnp.mean(x * x, axis=-1, keepdims=True) + RMS_EPS) * nw_ref[...]
    hb = h.astype(jnp.bfloat16)

    def split(i):
        return jnp.dot(hb, w_ref[:, i * SPLIT_W:(i + 1) * SPLIT_W], preferred_element_type=jnp.float32)

    def qk_norm(y, w):
        yy = (y * y).astype(jnp.bfloat16)
        ms = jnp.concatenate(
            [jnp.dot(yy[:, i * MXU_TILE:(i + 1) * MXU_TILE], g_ref[...], preferred_element_type=jnp.float32)
             for i in range(SPLIT_W // MXU_TILE)], axis=1)
        return y * lax.rsqrt(ms + RMS_EPS) * w

    q = qk_norm(split(0), qw_ref[...])
    q_ref[...] = (q * (ATTN_SCALE * LOG2E)).astype(jnp.bfloat16)
    tm = x.shape[0]

    def store_token_head_rows(o_ref, y):
        for hd in range(N_HEADS):
            o_ref[pl.ds(hd, tm, stride=N_HEADS), :] = y[:, hd * HEAD_DV:(hd + 1) * HEAD_DV]

    k = qk_norm(split(1), kw_ref[...])
    store_token_head_rows(k_ref, k)
    kb_ref[...] = k.astype(jnp.bfloat16)
    v = split(2)
    store_token_head_rows(v_ref, v)
    vb_ref[...] = v.astype(jnp.bfloat16)
    sga_ref[...] = _silu(split(3)).astype(jnp.bfloat16)

    z = split(5) * split(4)
    row = lax.broadcasted_iota(jnp.int32, z.shape, 0)
    r1 = pltpu.roll(z, 1, 0)
    r2 = pltpu.roll(z, 2, 0)
    if decode:
        t = row % DEC_SEQ
        p1 = jnp.where(t == 0, st1_ref[...], r1)
        p2 = jnp.where(t < 2, st2_ref[...], r2)
        tail_ref[...] = z
    else:
        first = pl.program_id(1) == 0
        zc = jnp.where(first, 0.0, zc_ref[...])
        p1 = jnp.where(row == 0, zc[7:8], r1)
        p2 = jnp.where(row == 0, zc[6:7], jnp.where(row == 1, zc[7:8], r2))
        zc_ref[...] = z[tm - 8:]
        tail_ref[...] = z[tm - 2:]
    cw = cw_ref[...]
    cy = cw[2:3] * z + cw[1:2] * p1 + cw[0:1] * p2
    c_ref[...] = (split(6) * cy * _silu(split(7))).astype(jnp.bfloat16)


def _project(x2d, norm_w, w_in_bf, g_mat, qw, kw, conv_w, *, batch, seq, decode, states=None):
    t_total = x2d.shape[0]
    tm = PROJ_TM
    assert seq % tm == 0 or decode
    if decode:
        assert t_total == tm
        grid = (1, 1)
        nt = 1
    else:
        nt = seq // tm
        grid = (batch, nt)
    tok = lambda b, t: (b * nt + t, 0)
    const = lambda b, t: (0, 0)
    in_specs = [
        pl.BlockSpec((tm, D_MODEL), tok),
        pl.BlockSpec((1, D_MODEL), const),
        pl.BlockSpec((D_MODEL, N_SPLITS * SPLIT_W), const),
        pl.BlockSpec((MXU_TILE, MXU_TILE), const),
        pl.BlockSpec((1, SPLIT_W), const),
        pl.BlockSpec((1, SPLIT_W), const),
        pl.BlockSpec((CONV_K, CONV_WIDTH), const),
    ]
    args = [x2d, norm_w, w_in_bf, g_mat, qw, kw, conv_w]
    if decode:
        in_specs += [pl.BlockSpec((tm, CONV_WIDTH), tok)] * 2
        args += list(states)
    f32, bf16 = jnp.float32, jnp.bfloat16
    tok_spec = pl.BlockSpec((tm, SPLIT_W), tok)
    th_spec = pl.BlockSpec((tm * N_HEADS, HEAD_DV), tok)
    th_shape = jax.ShapeDtypeStruct((t_total * N_HEADS, HEAD_DV), f32)
    out_shape = [jax.ShapeDtypeStruct((t_total, SPLIT_W), bf16), th_shape, th_shape] + [
        jax.ShapeDtypeStruct((t_total, SPLIT_W), bf16)] * 4
    out_specs = [tok_spec, th_spec, th_spec] + [tok_spec] * 4
    if decode:
        out_shape.append(jax.ShapeDtypeStruct((t_total, CONV_WIDTH), f32))
        out_specs.append(tok_spec)
        scratch = []
    else:
        out_shape.append(jax.ShapeDtypeStruct((batch, CONV_K - 1, CONV_WIDTH), f32))
        out_specs.append(pl.BlockSpec((None, CONV_K - 1, CONV_WIDTH), lambda b, t: (b, 0, 0)))
        scratch = [pltpu.VMEM((8, CONV_WIDTH), f32)]
    return pl.pallas_call(
        functools.partial(_proj_kernel, decode),
        grid=grid,
        in_specs=in_specs,
        out_specs=out_specs,
        out_shape=out_shape,
        scratch_shapes=scratch,
        compiler_params=pltpu.CompilerParams(
            dimension_semantics=("arbitrary", "arbitrary"), vmem_limit_bytes=VMEM_LIMIT_BYTES),
        name="proj_decode" if decode else "proj_prompt",
    )(*args)


def _diff_combine(o1, o2, lam, subln_w, lam_init):
    d = o1 - lam * o2
    d = d * lax.rsqrt(jnp.mean(d * d, axis=-1, keepdims=True) + RMS_EPS)
    return d * subln_w * (1.0 - lam_init)


def _attn_kernel(lam_init, n_chunks, past_len, layer_base,
                 pt_ref, q_ref, k_ref, v_ref, lq1_ref, lk1_ref, lq2_ref, lk2_ref, swc_ref, sw_ref,
                 x_ref, sga_ref, c_ref, wa_ref, wc_ref, qd_ref, kn_ref, vn_ref, ck_hbm, cv_hbm,
                 y_ref, ad_ref,
                 vt_ref, q2t_ref, sd_ref, m_ref, acc_ref, s0_ref, s1_ref, ag_ref,
                 kbuf, vbuf, sem, dm_ref, dl_ref, dacc_ref):
    b = pl.program_id(0)
    qi = pl.program_id(1)
    n_q = pl.num_programs(1)
    step = b * n_q + qi
    n_total = pl.num_programs(0) * n_q * n_chunks
    t = ATTN_T
    n_kt = vt_ref.shape[1]
    ppc = PAGES_PER_CHUNK
    f32, bf16 = jnp.float32, jnp.bfloat16

    def chunk_copies(g):
        slot = g % DECODE_RING
        copies = []
        for i in range(ppc):
            page = layer_base + pt_ref[g * ppc + i]
            copies.append(pltpu.make_async_copy(ck_hbm.at[page], kbuf.at[slot, i], sem.at[slot]))
            copies.append(pltpu.make_async_copy(cv_hbm.at[page], vbuf.at[slot, i], sem.at[slot]))
        return copies

    def start_chunk(g):
        @pl.when(g < n_total)
        def _():
            for cp in chunk_copies(g):
                cp.start()

    @pl.when(step == 0)
    def _():
        for g in range(DECODE_RING - 2):
            for cp in chunk_copies(g):
                cp.start()

    @pl.when(step == 0)
    def _():
        r = lax.broadcasted_iota(jnp.int32, (t, 2 * t), 0)
        c = lax.broadcasted_iota(jnp.int32, (t, 2 * t), 1) % t
        rel = (c - r).astype(f32)
        for h in range(N_HEADS):
            sd = (SLOPES[h] * LOG2E) * rel
            sd_ref[0, h] = sd
            sd_ref[1, h] = jnp.where(rel >= 0.0, sd, -NEG_BIG)
            vt_ref[h, :, HEAD_DV:, :] = jnp.ones((n_kt, ONES_ROWS, t), bf16)

    @pl.when(qi == 0)
    def _():
        for h in range(N_HEADS):
            for c in range(n_kt):
                blk = v_ref[c * t:(c + 1) * t, h * HEAD_DV:(h + 1) * HEAD_DV].astype(f32)
                vt_ref[h, c, :HEAD_DV, :] = blk.T.astype(bf16)

    drow = lax.broadcasted_iota(jnp.int32, (2 * HEAD_DQK, t), 0)
    for h in range(N_HEADS):
        qt = q_ref[:, h * HEAD_DV:(h + 1) * HEAD_DV].astype(f32).T
        q2t_ref[h] = jnp.concatenate(
            [jnp.where(drow < HEAD_DQK, qt, 0.0), jnp.where(drow >= HEAD_DQK, qt, 0.0)], axis=1).astype(bf16)
        m_ref[h] = jnp.full(m_ref.shape[1:], NEG_BIG, f32)
        acc_ref[h] = jnp.zeros(acc_ref.shape[1:], f32)

    def scores(j, s_ref):
        koff = pl.multiple_of(j * t, t)
        diag = (j == qi).astype(jnp.int32)
        for h in range(N_HEADS):
            kt = k_ref[pl.ds(koff, t), h * HEAD_DV:(h + 1) * HEAD_DV]
            s_ref[h] = jnp.dot(kt, q2t_ref[h], preferred_element_type=f32) - sd_ref[diag, h]

    def accumulate(j, s_ref):
        dist_tiles = (qi - j).astype(f32)
        for h in range(N_HEADS):
            s = s_ref[h]
            off = (SLOPES[h] * LOG2E * t) * dist_tiles
            m = m_ref[h]
            m_new = jnp.maximum(m, jnp.max(s, axis=0, keepdims=True) - off)
            p = jnp.exp2(s - (m_new + off))
            corr = jnp.exp2(m - m_new)
            pv = jnp.dot(vt_ref[h, j], p.astype(bf16), preferred_element_type=f32)
            acc_ref[h] = acc_ref[h] * corr + pv
            m_ref[h] = m_new

    n_rows = 2 * N_HEADS * DEC_SEQ
    n_keys = PAGE_SIZE * N_HEADS
    row = lax.broadcasted_iota(jnp.int32, (n_rows, 1), 0)
    row_head = (row // DEC_SEQ) % N_HEADS
    row_tok = row % DEC_SEQ
    row_slope = jnp.where(row_head == 0, SLOPES[0], jnp.where(
        row_head == 1, SLOPES[1], jnp.where(row_head == 2, SLOPES[2], SLOPES[3]))).astype(f32) * LOG2E

    def online_update(s_list, v_list, off):
        m = dm_ref[...]
        mt = functools.reduce(jnp.maximum, [jnp.max(s, axis=-1, keepdims=True) for s in s_list])
        m_new = jnp.maximum(m, mt - off)
        shift = m_new + off
        corr = jnp.exp2(m - m_new)
        lsum = jnp.zeros_like(m)
        pv = jnp.zeros(dacc_ref.shape, f32)
        for s, vb in zip(s_list, v_list):
            p = jnp.exp2(s - shift)
            lsum = lsum + jnp.sum(p, axis=-1, keepdims=True)
            pv = pv + jnp.dot(p.astype(bf16), vb, preferred_element_type=f32)
        dm_ref[...] = m_new
        dl_ref[...] = dl_ref[...] * corr + lsum
        dacc_ref[...] = dacc_ref[...] * corr + pv

    qd = qd_ref[...]
    key = lax.broadcasted_iota(jnp.int32, (n_rows, n_keys), 1)
    key_head = key % N_HEADS
    key_tok = key // N_HEADS
    rel = (row_tok - key_tok).astype(f32)
    bias0 = jnp.where(key_head == row_head, -row_slope * rel, NEG_BIG)

    dm_ref[...] = jnp.full(dm_ref.shape, NEG_BIG, f32)
    dl_ref[...] = jnp.zeros(dl_ref.shape, f32)
    dacc_ref[...] = jnp.zeros(dacc_ref.shape, f32)

    nk = DEC_SEQ * N_HEADS
    kcol = lax.broadcasted_iota(jnp.int32, (n_rows, nk), 1)
    kh = kcol % N_HEADS
    ktok = kcol // N_HEADS
    ok = (kh == row_head) & (ktok <= row_tok)
    bias_new = jnp.where(ok, -row_slope * (row_tok - ktok).astype(f32), NEG_BIG)
    s_new = lax.dot_general(qd, kn_ref[...].astype(bf16), _NT, preferred_element_type=f32)
    online_update([s_new + bias_new], [vn_ref[...].astype(bf16)], jnp.zeros((n_rows, 1), f32))

    first_chunk = step * n_chunks

    def start_pair(ci):
        start_chunk(first_chunk + ci + (DECODE_RING - 2))
        start_chunk(first_chunk + ci + (DECODE_RING - 1))

    def wait_chunk(ci):
        for cp in chunk_copies(first_chunk + ci):
            cp.wait()

    def consume_chunk(ci):
        slot = (first_chunk + ci) % DECODE_RING
        s_list, v_list = [], []
        for i in range(ppc):
            kb = kbuf[slot, i].astype(bf16)
            s = lax.dot_general(qd, kb, _NT, preferred_element_type=f32)
            s_list.append(s + (bias0 + row_slope * float(i * PAGE_SIZE)))
            v_list.append(vbuf[slot, i].astype(bf16))
        first_pos = lax.convert_element_type(ci * (ppc * PAGE_SIZE), f32)
        online_update(s_list, v_list, row_slope * (float(past_len) - first_pos))

    scores(0, s0_ref)
    n_pairs = (qi + 1) // 2

    def pair(i, carry):
        j = 2 * i
        start_pair(j)
        wait_chunk(j)
        wait_chunk(j + 1)
        scores(j + 1, s1_ref)
        accumulate(j, s0_ref)
        consume_chunk(j)
        scores(jnp.minimum(j + 2, qi), s0_ref)
        accumulate(j + 1, s1_ref)
        consume_chunk(j + 1)
        return carry

    lax.fori_loop(0, n_pairs, pair, 0)

    @pl.when(qi % 2 == 0)
    def _():
        accumulate(qi, s0_ref)

    def decode_pair(i, carry):
        j = 2 * i
        start_pair(j)
        wait_chunk(j)
        wait_chunk(j + 1)
        consume_chunk(j)
        consume_chunk(j + 1)
        return carry

    lax.fori_loop(n_pairs, n_chunks // 2, decode_pair, 0)

    lam = _lambda_value(lq1_ref[...], lk1_ref[...], lq2_ref[...], lk2_ref[...], lam_init)
    for h in range(N_HEADS):
        acc = acc_ref[h]
        o = acc[:HEAD_DV] / acc[HEAD_DV:HEAD_DV + 1]
        d = o[:, :t] - lam * o[:, t:]
        d = d * lax.rsqrt(jnp.mean(d * d, axis=0, keepdims=True) + RMS_EPS)
        d = d * swc_ref[...] * (1.0 - lam_init)
        hs = slice(h * HEAD_DV, (h + 1) * HEAD_DV)
        ag_ref[:, hs] = (d.T * sga_ref[:, hs].astype(f32)).astype(bf16)

    y = jnp.dot(ag_ref[...], wa_ref[...], preferred_element_type=f32)
    y = y + jnp.dot(c_ref[...], wc_ref[...], preferred_element_type=f32)
    y_ref[...] = x_ref[...] + y

    od = dacc_ref[...] / dl_ref[...]
    half = n_rows // 2
    ad_ref[...] = _diff_combine(od[:half], od[half:], lam, sw_ref[...], lam_init)


def _attention(page_table, q_bf, k_bf, v_bf, lq1, lk1, lq2, lk2, subln_w, x2d, sga, c, wa, wc,
               qcat, k_new, v_new, cache_k3, cache_v3, *, batch, seq, layer_base, lam_init):
    t = ATTN_T
    assert seq % t == 0
    nq = seq // t
    dec_batch, n_pages = page_table.shape
    assert dec_batch == batch * nq
    ppc = PAGES_PER_CHUNK
    assert n_pages % ppc == 0
    n_chunks = n_pages // ppc
    assert n_chunks % 2 == 0 and nq // 2 <= n_chunks // 2
    assert dec_batch * n_chunks >= DECODE_RING - 2
    n_keys = PAGE_SIZE * N_HEADS
    n_rows = qcat.shape[1]
    f32, bf16 = jnp.float32, jnp.bfloat16

    tile = lambda w: pl.BlockSpec((t, w), lambda b, i, pt: (b * nq + i, 0))
    whole_seq = pl.BlockSpec((seq, ATTN_WIDTH), lambda b, i, pt: (b, 0), pipeline_mode=pl.Buffered(1))
    const = lambda shape: pl.BlockSpec(shape, lambda b, i, pt: (0, 0))
    per_step = lambda rows: pl.BlockSpec((None, rows, HEAD_DV), lambda b, i, pt: (b * nq + i, 0, 0))
    vec = const((1, HEAD_DQK))
    in_specs = [
        tile(ATTN_WIDTH), whole_seq, whole_seq, vec, vec, vec, vec,
        const((HEAD_DV, 1)), const((1, HEAD_DV)),
        tile(D_MODEL), tile(ATTN_WIDTH), tile(CONV_WIDTH),
        const((ATTN_WIDTH, D_MODEL)), const((CONV_WIDTH, D_MODEL)),
        per_step(n_rows), per_step(DEC_SEQ * N_HEADS), per_step(DEC_SEQ * N_HEADS),
        pl.BlockSpec(memory_space=pl.ANY), pl.BlockSpec(memory_space=pl.ANY),
    ]
    grid_spec = pltpu.PrefetchScalarGridSpec(
        num_scalar_prefetch=1,
        grid=(batch, nq),
        in_specs=in_specs,
        out_specs=[tile(D_MODEL), per_step(n_rows // 2)],
        scratch_shapes=[
            pltpu.VMEM((N_HEADS, nq, HEAD_DV + ONES_ROWS, t), bf16),
            pltpu.VMEM((N_HEADS, 2 * HEAD_DQK, 2 * t), bf16),
            pltpu.VMEM((2, N_HEADS, t, 2 * t), f32),
            pltpu.VMEM((N_HEADS, 1, 2 * t), f32),
            pltpu.VMEM((N_HEADS, HEAD_DV + ONES_ROWS, 2 * t), f32),
            pltpu.VMEM((N_HEADS, t, 2 * t), f32),
            pltpu.VMEM((N_HEADS, t, 2 * t), f32),
            pltpu.VMEM((t, ATTN_WIDTH), bf16),
            pltpu.VMEM((DECODE_RING, ppc, n_keys, HEAD_DV), f32),
            pltpu.VMEM((DECODE_RING, ppc, n_keys, HEAD_DV), f32),
            pltpu.SemaphoreType.DMA((DECODE_RING,)),
            pltpu.VMEM((n_rows, 1), f32),
            pltpu.VMEM((n_rows, 1), f32),
            pltpu.VMEM((n_rows, HEAD_DV), f32),
        ],
    )
    return pl.pallas_call(
        functools.partial(_attn_kernel, lam_init, n_chunks, n_pages * PAGE_SIZE, layer_base),
        grid_spec=grid_spec,
        out_shape=[jax.ShapeDtypeStruct((batch * seq, D_MODEL), f32),
                   jax.ShapeDtypeStruct((dec_batch, n_rows // 2, HEAD_DV), f32)],
        compiler_params=pltpu.CompilerParams(
            dimension_semantics=("arbitrary", "arbitrary"), vmem_limit_bytes=VMEM_LIMIT_BYTES),
        name="attn_fused",
    )(page_table.reshape(-1), q_bf, k_bf, v_bf, lq1, lk1, lq2, lk2,
      subln_w.reshape(HEAD_DV, 1), subln_w.reshape(1, HEAD_DV), x2d, sga, c, wa, wc,
      qcat, k_new, v_new, cache_k3, cache_v3)


def _out_kernel(x_ref, a_ref, sga_ref, c_ref, wa_ref, wc_ref, y_ref):
    ag = (a_ref[...].astype(jnp.float32) * sga_ref[...].astype(jnp.float32)).astype(jnp.bfloat16)
    y = jnp.dot(ag, wa_ref[...], preferred_element_type=jnp.float32)
    y = y + jnp.dot(c_ref[...], wc_ref[...], preferred_element_type=jnp.float32)
    y_ref[...] = x_ref[...] + y


def _out_project(x2d, a, sga, c, wa, wc):
    t_total = x2d.shape[0]
    tm = PROJ_TM
    assert t_total % tm == 0
    tok = lambda i: (i, 0)
    const = lambda i: (0, 0)
    return pl.pallas_call(
        _out_kernel,
        grid=(t_total // tm,),
        in_specs=[
            pl.BlockSpec((tm, D_MODEL), tok),
            pl.BlockSpec((tm, ATTN_WIDTH), tok),
            pl.BlockSpec((tm, ATTN_WIDTH), tok),
            pl.BlockSpec((tm, CONV_WIDTH), tok),
            pl.BlockSpec((ATTN_WIDTH, D_MODEL), const),
            pl.BlockSpec((CONV_WIDTH, D_MODEL), const),
        ],
        out_specs=pl.BlockSpec((tm, D_MODEL), tok),
        out_shape=jax.ShapeDtypeStruct((t_total, D_MODEL), jnp.float32),
        compiler_params=pltpu.CompilerParams(
            dimension_semantics=("arbitrary",), vmem_limit_bytes=VMEM_LIMIT_BYTES),
        name="out_proj",
    )(x2d, a, sga, c, wa, wc)


def _group_mean_matrix():
    g = np.kron(np.eye(MXU_TILE // HEAD_DQK, dtype=np.float32),
                np.full((HEAD_DQK, HEAD_DQK), 1.0 / HEAD_DQK, np.float32))
    return jnp.asarray(g, jnp.bfloat16)


def kernel(x_prompt, x_sample, cache_k, cache_v, state_conv, page_table, norm_w, w_in, q_norm_w, k_norm_w,
           lambda_q1, lambda_k1, lambda_q2, lambda_k2, subln_w, conv_w, w_out):
    depth = norm_w.shape[0]
    batch, seq, _ = x_prompt.shape
    dec_batch, dec_seq, _ = x_sample.shape
    assert dec_seq == DEC_SEQ
    n_pool = cache_k.shape[1]
    n_keys = PAGE_SIZE * N_HEADS
    f32 = jnp.float32
    g_mat = _group_mean_matrix()
    cache_k3 = cache_k.reshape(depth * n_pool, n_keys, HEAD_DV)
    cache_v3 = cache_v.reshape(depth * n_pool, n_keys, HEAD_DV)

    xp = x_prompt.reshape(batch * seq, D_MODEL)
    xs = x_sample.reshape(dec_batch * dec_seq, D_MODEL)
    outs = [[] for _ in range(6)]
    for layer in range(depth):
        lam_init = _lambda_init(layer)
        w_in_bf = w_in[layer].astype(jnp.bfloat16)
        w_out_bf = w_out[layer].astype(jnp.bfloat16)
        wa, wc = w_out_bf[:ATTN_WIDTH], w_out_bf[ATTN_WIDTH:]
        nw = norm_w[layer].reshape(1, D_MODEL)
        qw = jnp.tile(q_norm_w[layer], SPLIT_W // HEAD_DQK).reshape(1, SPLIT_W)
        kw = jnp.tile(k_norm_w[layer], SPLIT_W // HEAD_DQK).reshape(1, SPLIT_W)
        lvecs = [v[layer].reshape(1, HEAD_DQK).astype(f32) for v in (lambda_q1, lambda_k1, lambda_q2, lambda_k2)]
        sw = subln_w[layer].astype(f32)
        cw = conv_w[layer]

        q_bf, k, v, k_bf, v_bf, sga, c, conv_tail = _project(
            xp, nw, w_in_bf, g_mat, qw, kw, cw, batch=batch, seq=seq, decode=False)
        st = state_conv[layer].astype(f32)
        zeros = jnp.zeros((dec_batch, 1, CONV_WIDTH), f32)
        st1 = jnp.concatenate([st[:, 1:2], zeros, zeros, zeros], axis=1).reshape(-1, CONV_WIDTH)
        st2 = jnp.concatenate([st[:, 0:1], st[:, 1:2], zeros, zeros], axis=1).reshape(-1, CONV_WIDTH)
        qs_bf, ks, vs, _, _, sga_s, c_s, z = _project(
            xs, nw, w_in_bf, g_mat, qw, kw, cw, batch=dec_batch, seq=dec_seq, decode=True, states=(st1, st2))
        q5 = qs_bf.reshape(dec_batch, dec_seq, N_HEADS, 2, HEAD_DQK).transpose(0, 3, 2, 1, 4)
        eye = jnp.eye(2, dtype=qs_bf.dtype)
        qcat = (q5[:, :, :, :, None, :] * eye[None, :, None, None, :, None]).reshape(
            dec_batch, 2 * N_HEADS * dec_seq, 2 * HEAD_DQK)

        xp, a_s = _attention(
            page_table, q_bf, k_bf, v_bf, *lvecs, sw, xp, sga, c, wa, wc, qcat,
            ks.reshape(dec_batch, dec_seq * N_HEADS, HEAD_DV), vs.reshape(dec_batch, dec_seq * N_HEADS, HEAD_DV),
            cache_k3, cache_v3, batch=batch, seq=seq, layer_base=layer * n_pool, lam_init=lam_init)
        a_s = a_s.reshape(dec_batch, N_HEADS, dec_seq, HEAD_DV).transpose(0, 2, 1, 3).reshape(-1, ATTN_WIDTH)
        xs = _out_project(xs, a_s, sga_s, c_s, wa, wc)

        outs[0].append(k.reshape(batch, seq, N_HEADS, 2 * HEAD_DQK))
        outs[1].append(v.reshape(batch, seq, N_HEADS, HEAD_DV))
        outs[2].append(conv_tail)
        outs[3].append(ks.reshape(dec_batch, dec_seq, N_HEADS, 2 * HEAD_DQK))
        outs[4].append(vs.reshape(dec_batch, dec_seq, N_HEADS, HEAD_DV))
        outs[5].append(z.reshape(dec_batch, dec_seq, CONV_WIDTH)[:, dec_seq - (CONV_K - 1):])

    return (xp.reshape(batch, seq, D_MODEL), xs.reshape(dec_batch, dec_seq, D_MODEL),
            jnp.stack(outs[0]), jnp.stack(outs[1]), jnp.stack(outs[2]),
            jnp.stack(outs[3]), jnp.stack(outs[4]), jnp.stack(outs[5]))
```
